```python
import math
import jax, jax.numpy as jnp
from jax import lax
import numpy as np

D_MODEL = 4096
BATCH = 4
SEQ = 2048
DEPTH = 2
DEC_BATCH = 8
DEC_SEQ = 8
PAST_LEN = 16384
PAGE_SIZE = 128

HEAD_DIM = 128
CONV_CH = D_MODEL // 2
CONV_WIDTH = 3
FOX_HEADS = (D_MODEL // 2) // HEAD_DIM
FOX_WIDTH = FOX_HEADS * HEAD_DIM
FOX_Q_BLOCK = 128
FORGET_BIAS_INIT = 4.0
MOBA_HEADS = D_MODEL // HEAD_DIM
MOBA_KV_HEADS = MOBA_HEADS // 4
MOBA_Q_WIDTH = MOBA_HEADS * HEAD_DIM
MOBA_KV_WIDTH = MOBA_KV_HEADS * HEAD_DIM
MOBA_BLOCK = 256
MOBA_TOPK = 3
MOBA_Q_CHUNK = 16
MEM_LEN = 256
CROSS_HEADS = 4
CROSS_HEAD_DIM = 256
CROSS_WIDTH = CROSS_HEADS * CROSS_HEAD_DIM
D_FF = 256 * ((8 * D_MODEL // 3 + 255) // 256)
ROPE_THETA = 10000.0
ALPHA = (2 * DEPTH) ** 0.25
BETA = (8 * DEPTH) ** -0.25
LN_EPS = 1e-5
ATTN_SCALE = HEAD_DIM ** -0.5
CROSS_SCALE = CROSS_HEAD_DIM ** -0.5
N_EVEN = (DEPTH + 1) // 2
N_ODD = DEPTH // 2

kernel_name = 'hybrid_conv_fox_moba_decoder_step'


def layer_norm(x, g, b):
    xf = x.astype(jnp.float32)
    mu = jnp.mean(xf, axis=-1, keepdims=True)
    var = jnp.mean(jnp.square(xf - mu), axis=-1, keepdims=True)
    y = (xf - mu) * lax.rsqrt(var + LN_EPS) * g.astype(jnp.float32) + b.astype(jnp.float32)
    return y.astype(x.dtype)


def post_norm(x, sub, g, b):
    return layer_norm(ALPHA * x + sub, g, b)


def swiglu(x, w_gate, w_up, w_down):
    return (jax.nn.silu(x @ w_gate) * (x @ w_up)) @ w_down


def rope(x, pos):
    half = HEAD_DIM // 2
    inv_freq = ROPE_THETA ** (-jnp.arange(half, dtype=jnp.float32) / half)
    ang = pos.astype(jnp.float32)[:, None] * inv_freq[None, :]
    cos = jnp.cos(ang)[None, :, None, :]
    sin = jnp.sin(ang)[None, :, None, :]
    xf = x.astype(jnp.float32)
    x1, x2 = xf[..., :half], xf[..., half:]
    return jnp.concatenate([x1 * cos - x2 * sin, x1 * sin + x2 * cos], axis=-1).astype(x.dtype)


def gather_pages(pool, page_table):
    g = pool[page_table]
    return g.reshape((g.shape[0], g.shape[1] * g.shape[2]) + g.shape[3:])


def pad_to_blocks(x):
    pad = (-x.shape[1]) % MOBA_BLOCK
    return jnp.pad(x, ((0, 0), (0, pad), (0, 0), (0, 0)))


def causal_short_conv(u, buf, w):
    T = u.shape[1]
    ext = jnp.concatenate([buf.astype(u.dtype), u], axis=1)
    y = w[0] * ext[:, 0:T]
    for i in range(1, CONV_WIDTH):
        y = y + w[i] * ext[:, i:i + T]
    return y, ext[:, T:]


def fox_prompt(q, k, v, logf):
    B, T, H, _ = q.shape
    nblk = T // FOX_Q_BLOCK
    cum = jnp.cumsum(logf, axis=1).transpose(0, 2, 1)
    kpos = jnp.arange(T)
    qb = q.reshape(B, nblk, FOX_Q_BLOCK, H, HEAD_DIM).swapaxes(0, 1)
    cb = cum.reshape(B, H, nblk, FOX_Q_BLOCK).transpose(2, 0, 1, 3)

    def block(args):
        i, q_i, c_i = args
        qpos = i * FOX_Q_BLOCK + jnp.arange(FOX_Q_BLOCK)
        s = jnp.einsum('bqhd,bkhd->bhqk', q_i, k, preferred_element_type=jnp.float32) * ATTN_SCALE
        s = s + (c_i[..., :, None] - cum[..., None, :])
        s = jnp.where(kpos[None, :] <= qpos[:, None], s, -jnp.inf)
        p = jax.nn.softmax(s, axis=-1).astype(v.dtype)
        return jnp.einsum('bhqk,bkhd->bqhd', p, v)

    o = lax.map(block, (jnp.arange(nblk), qb, cb))
    return o.swapaxes(0, 1).reshape(B, T, H, HEAD_DIM)


def fox_sample(q, k, v, logf, k_past, v_past, logf_past):
    T = q.shape[1]
    P = k_past.shape[1]
    lp = logf_past.astype(jnp.float32)
    suffix = (lax.cumsum(lp, axis=1, reverse=True) - lp).transpose(0, 2, 1)
    cum = jnp.cumsum(logf, axis=1).transpose(0, 2, 1)
    s_past = jnp.einsum('bqhd,bkhd->bhqk', q, k_past.astype(q.dtype), preferred_element_type=jnp.float32) * ATTN_SCALE
    s_past = s_past + cum[..., :, None] + suffix[..., None, :]
    s_new = jnp.einsum('bqhd,bkhd->bhqk', q, k, preferred_element_type=jnp.float32) * ATTN_SCALE
    s_new = s_new + cum[..., :, None] - cum[..., None, :]
    causal = jnp.arange(T)[None, :] <= jnp.arange(T)[:, None]
    s_new = jnp.where(causal, s_new, -jnp.inf)
    p = jax.nn.softmax(jnp.concatenate([s_past, s_new], axis=-1), axis=-1).astype(v.dtype)
    return (jnp.einsum('bhqk,bkhd->bqhd', p[..., :P], v_past.astype(v.dtype))
            + jnp.einsum('bhqk,bkhd->bqhd', p[..., P:], v))


def moba_attend(q, k_all, v_all, q_pos):
    B, Tq, H, _ = q.shape
    G = H // MOBA_KV_HEADS
    nb = k_all.shape[1] // MOBA_BLOCK
    kb = k_all.reshape(B, nb, MOBA_BLOCK, MOBA_KV_HEADS, HEAD_DIM).transpose(0, 3, 1, 2, 4)
    vb = v_all.reshape(B, nb, MOBA_BLOCK, MOBA_KV_HEADS, HEAD_DIM).transpose(0, 3, 1, 2, 4)
    k_mean = jnp.mean(kb.astype(jnp.float32), axis=3)
    qg = q.reshape(B, Tq, MOBA_KV_HEADS, G, HEAD_DIM)
    gate = jnp.einsum('btgrd,bgnd->btgn', qg.astype(jnp.float32), k_mean)
    own = q_pos // MOBA_BLOCK
    gate = jnp.where(jnp.arange(nb)[None, None, None, :] < own[None, :, None, None], gate, -jnp.inf)
    n_sel = min(MOBA_TOPK, nb)
    _, top = lax.top_k(gate, n_sel)
    blk = jnp.concatenate([top.astype(jnp.int32),
                           jnp.broadcast_to(own[None, :, None, None], (B, Tq, MOBA_KV_HEADS, 1)).astype(jnp.int32)], axis=-1)
    S = n_sel + 1
    blk_ok = jnp.concatenate([jnp.arange(n_sel)[None, :] < own[:, None],
                              jnp.ones((Tq, 1), dtype=bool)], axis=-1)
    qc = math.gcd(Tq, MOBA_Q_CHUNK)
    nc = Tq // qc
    b_ix = jnp.arange(B)[:, None, None, None]
    g_ix = jnp.arange(MOBA_KV_HEADS)[None, None, :, None]

    def chunk(args):
        q_c, blk_c, ok_c, pos_c = args
        k_g = kb[b_ix, g_ix, blk_c]
        v_g = vb[b_ix, g_ix, blk_c]
        kpos = blk_c[..., None] * MOBA_BLOCK + jnp.arange(MOBA_BLOCK)
        mask = ok_c[None, :, None, :, None] & (kpos <= pos_c[None, :, None, None, None])
        s = jnp.einsum('bqgrd,bqgskd->bqgrsk', q_c, k_g, preferred_element_type=jnp.float32) * ATTN_SCALE
        s = jnp.where(mask[:, :, :, None], s, -jnp.inf).reshape(B, qc, MOBA_KV_HEADS, G, S * MOBA_BLOCK)
        p = jax.nn.softmax(s, axis=-1).reshape(B, qc, MOBA_KV_HEADS, G, S, MOBA_BLOCK).astype(v_all.dtype)
        return jnp.einsum('bqgrsk,bqgskd->bqgrd', p, v_g)

    xs = (qg.reshape(B, nc, qc, MOBA_KV_HEADS, G, HEAD_DIM).swapaxes(0, 1),
          blk.reshape(B, nc, qc, MOBA_KV_HEADS, S).swapaxes(0, 1),
          blk_ok.reshape(nc, qc, S),
          q_pos.reshape(nc, qc))
    o = lax.map(chunk, xs)
    return o.swapaxes(0, 1).reshape(B, Tq, H, HEAD_DIM)


def cross_attend(x, mem_k, mem_v, w_q, w_o):
    B, T, _ = x.shape
    q = (x @ w_q).reshape(B, T, CROSS_HEADS, CROSS_HEAD_DIM)
    s = jnp.einsum('bthd,bmhd->bhtm', q, mem_k, preferred_element_type=jnp.float32) * CROSS_SCALE
    p = jax.nn.softmax(s, axis=-1).astype(mem_v.dtype)
    o = jnp.einsum('bhtm,bmhd->bthd', p, mem_v)
    return o.reshape(B, T, CROSS_WIDTH) @ w_o


def even_mixer(x, j, past, p):
    B, T, _ = x.shape
    cuts = [CONV_CH, 2 * CONV_CH, 3 * CONV_CH, 3 * CONV_CH + FOX_WIDTH,
            3 * CONV_CH + 2 * FOX_WIDTH, 3 * CONV_CH + 3 * FOX_WIDTH]
    a_in, a_b, a_c, q, k, v, f = jnp.split(x @ p['w_in_even'][j], cuts, axis=-1)
    u = a_c * a_in
    if past is None:
        buf = jnp.zeros((B, CONV_WIDTH - 1, CONV_CH), dtype=u.dtype)
    else:
        buf = past['state_conv'][j]
    y, new_buf = causal_short_conv(u, buf, p['w_conv'][j])
    a_out = a_b * y
    q = q.reshape(B, T, FOX_HEADS, HEAD_DIM)
    k = k.reshape(B, T, FOX_HEADS, HEAD_DIM)
    v = v.reshape(B, T, FOX_HEADS, HEAD_DIM)
    logf = jax.nn.log_sigmoid(f.astype(jnp.float32) + p['b_forget'][j].astype(jnp.float32))
    if past is None:
        o = fox_prompt(q, k, v, logf)
    else:
        pt = past['page_table']
        o = fox_sample(q, k, v, logf,
                       gather_pages(past['cache_fox_k'][j], pt),
                       gather_pages(past['cache_fox_v'][j], pt),
                       gather_pages(past['cache_fox_logf'][j], pt))
    out = jnp.concatenate([a_out, o.reshape(B, T, FOX_WIDTH)], axis=-1) @ p['w_out_even'][j]
    return out, {'fox_k': k, 'fox_v': v, 'fox_logf': logf, 'conv': new_buf}


def odd_mixer(x, pos, j, past, p):
    B, T, _ = x.shape
    q, k, v = jnp.split(x @ p['w_in_odd'][j], [MOBA_Q_WIDTH, MOBA_Q_WIDTH + MOBA_KV_WIDTH], axis=-1)
    q = rope(q.reshape(B, T, MOBA_HEADS, HEAD_DIM), pos)
    k = rope(k.reshape(B, T, MOBA_KV_HEADS, HEAD_DIM), pos)
    v = v.reshape(B, T, MOBA_KV_HEADS, HEAD_DIM)
    if past is None:
        k_all, v_all = k, v
    else:
        pt = past['page_table']
        k_all = jnp.concatenate([gather_pages(past['cache_moba_k'][j], pt).astype(k.dtype), k], axis=1)
        v_all = jnp.concatenate([gather_pages(past['cache_moba_v'][j], pt).astype(v.dtype), v], axis=1)
    o = moba_attend(q, pad_to_blocks(k_all), pad_to_blocks(v_all), pos)
    return o.reshape(B, T, MOBA_Q_WIDTH) @ p['w_out_odd'][j], {'moba_k': k, 'moba_v': v}


def trunk(x, pos, mem_k, mem_v, past, p):
    st = {'fox_k': [], 'fox_v': [], 'fox_logf': [], 'conv': [], 'moba_k': [], 'moba_v': []}
    for i in range(DEPTH):
        g, b = p['ln_g'][i], p['ln_b'][i]
        x = post_norm(x, 0.5 * swiglu(x, p['w_ffa_gate'][i], p['w_ffa_up'][i], p['w_ffa_down'][i]), g[0], b[0])
        if i % 2 == 0:
            mix, new = even_mixer(x, i // 2, past, p)
        else:
            mix, new = odd_mixer(x, pos, i // 2, past, p)
        for name, val in new.items():
            st[name].append(val)
        x = post_norm(x, mix, g[1], b[1])
        x = post_norm(x, cross_attend(x, mem_k[i], mem_v[i], p['w_cross_q'][i], p['w_cross_o'][i]), g[2], b[2])
        x = post_norm(x, 0.5 * swiglu(x, p['w_ffb_gate'][i], p['w_ffb_up'][i], p['w_ffb_down'][i]), g[3], b[3])
    return x, {name: jnp.stack(vals) for name, vals in st.items()}


def setup_inputs(seed: int = 0) -> dict:
    key = jax.random.key(seed)
    ks = iter(jax.random.split(key, 48))

    def nrm(shape, scale):
        return scale * jax.random.normal(next(ks), shape, dtype=jnp.float32)

    n_pages = PAST_LEN // PAGE_SIZE
    used = DEC_BATCH * n_pages
    n_phys = used + (used + 3) // 4
    page_table = jax.random.permutation(next(ks), n_phys)[:used].reshape(DEC_BATCH, n_pages).astype(jnp.int32)
    in_even = 3 * CONV_CH + 3 * FOX_WIDTH + FOX_HEADS
    in_odd = MOBA_Q_WIDTH + 2 * MOBA_KV_WIDTH
    sd = D_MODEL ** -0.5
    return {
        'x_prompt': nrm((BATCH, SEQ, D_MODEL), 1.0),
        'x_sample': nrm((DEC_BATCH, DEC_SEQ, D_MODEL), 1.0),
        'cache_fox_k': nrm((N_EVEN, n_phys, PAGE_SIZE, FOX_HEADS, HEAD_DIM), 1.0),
        'cache_fox_v': nrm((N_EVEN, n_phys, PAGE_SIZE, FOX_HEADS, HEAD_DIM), 1.0),
        'cache_fox_logf': jax.nn.log_sigmoid(FORGET_BIAS_INIT + nrm((N_EVEN, n_phys, PAGE_SIZE, FOX_HEADS), 1.0)),
        'state_conv': nrm((N_EVEN, DEC_BATCH, CONV_WIDTH - 1, CONV_CH), 1.0),
        'cache_moba_k': nrm((N_ODD, n_phys, PAGE_SIZE, MOBA_KV_HEADS, HEAD_DIM), 1.0),
        'cache_moba_v': nrm((N_ODD, n_phys, PAGE_SIZE, MOBA_KV_HEADS, HEAD_DIM), 1.0),
        'cache_mem_k': nrm((DEPTH, DEC_BATCH, MEM_LEN, CROSS_HEADS, CROSS_HEAD_DIM), 1.0),
        'cache_mem_v': nrm((DEPTH, DEC_BATCH, MEM_LEN, CROSS_HEADS, CROSS_HEAD_DIM), 1.0),
        'page_table': page_table,
        'mem_prompt': nrm((BATCH, MEM_LEN, D_MODEL), 1.0),
        'w_in_even': nrm((N_EVEN, D_MODEL, in_even), sd),
        'b_forget': FORGET_BIAS_INIT + nrm((N_EVEN, FOX_HEADS), 0.5),
        'w_conv': nrm((N_EVEN, CONV_WIDTH, CONV_CH), CONV_WIDTH ** -0.5),
        'w_out_even': nrm((N_EVEN, CONV_CH + FOX_WIDTH, D_MODEL), BETA * (CONV_CH + FOX_WIDTH) ** -0.5),
        'w_in_odd': nrm((N_ODD, D_MODEL, in_odd), sd),
        'w_out_odd': nrm((N_ODD, MOBA_Q_WIDTH, D_MODEL), BETA * MOBA_Q_WIDTH ** -0.5),
        'w_cross_q': nrm((DEPTH, D_MODEL, CROSS_WIDTH), sd),
        'w_cross_k': nrm((DEPTH, D_MODEL, CROSS_WIDTH), sd),
        'w_cross_v': nrm((DEPTH, D_MODEL, CROSS_WIDTH), sd),
        'w_cross_o': nrm((DEPTH, CROSS_WIDTH, D_MODEL), BETA * CROSS_WIDTH ** -0.5),
        'w_ffa_gate': nrm((DEPTH, D_MODEL, D_FF), sd),
        'w_ffa_up': nrm((DEPTH, D_MODEL, D_FF), sd),
        'w_ffa_down': nrm((DEPTH, D_FF, D_MODEL), BETA * D_FF ** -0.5),
        'w_ffb_gate': nrm((DEPTH, D_MODEL, D_FF), sd),
        'w_ffb_up': nrm((DEPTH, D_MODEL, D_FF), sd),
        'w_ffb_down': nrm((DEPTH, D_FF, D_MODEL), BETA * D_FF ** -0.5),
        'ln_g': 1.0 + nrm((DEPTH, 4, D_MODEL), 0.05),
        'ln_b': nrm((DEPTH, 4, D_MODEL), 0.02),
    }


def reference(x_prompt, x_sample, cache_fox_k, cache_fox_v, cache_fox_logf, state_conv,
              cache_moba_k, cache_moba_v, cache_mem_k, cache_mem_v, page_table, mem_prompt,
              w_in_even, b_forget, w_conv, w_out_even, w_in_odd, w_out_odd,
              w_cross_q, w_cross_k, w_cross_v, w_cross_o,
              w_ffa_gate, w_ffa_up, w_ffa_down, w_ffb_gate, w_ffb_up, w_ffb_down, ln_g, ln_b):
    params = {
        'w_in_even': w_in_even, 'b_forget': b_forget, 'w_conv': w_conv, 'w_out_even': w_out_even,
        'w_in_odd': w_in_odd, 'w_out_odd': w_out_odd,
        'w_cross_q': w_cross_q, 'w_cross_o': w_cross_o,
        'w_ffa_gate': w_ffa_gate, 'w_ffa_up': w_ffa_up, 'w_ffa_down': w_ffa_down,
        'w_ffb_gate': w_ffb_gate, 'w_ffb_up': w_ffb_up, 'w_ffb_down': w_ffb_down,
        'ln_g': ln_g, 'ln_b': ln_b,
    }
    bp = mem_prompt.shape[0]
    mem_k_p = jnp.einsum('bmd,ldc->lbmc', mem_prompt, w_cross_k).reshape(DEPTH, bp, MEM_LEN, CROSS_HEADS, CROSS_HEAD_DIM)
    mem_v_p = jnp.einsum('bmd,ldc->lbmc', mem_prompt, w_cross_v).reshape(DEPTH, bp, MEM_LEN, CROSS_HEADS, CROSS_HEAD_DIM)
    pos_p = jnp.arange(x_prompt.shape[1], dtype=jnp.int32)
    y_prompt, st_p = trunk(x_prompt, pos_p, mem_k_p, mem_v_p, None, params)
    past = {
        'page_table': page_table, 'cache_fox_k': cache_fox_k, 'cache_fox_v': cache_fox_v,
        'cache_fox_logf': cache_fox_logf, 'state_conv': state_conv,
        'cache_moba_k': cache_moba_k, 'cache_moba_v': cache_moba_v,
    }
    past_len = page_table.shape[1] * PAGE_SIZE
    pos_s = past_len + jnp.arange(x_sample.shape[1], dtype=jnp.int32)
    y_sample, st_s = trunk(x_sample, pos_s, cache_mem_k, cache_mem_v, past, params)
    return (y_prompt, y_sample,
            st_p['fox_k'], st_p['fox_v'], st_p['fox_logf'], st_p['conv'], st_p['moba_k'], st_p['moba_v'],
            mem_k_p, mem_v_p,
            st_s['fox_k'], st_s['fox_v'], st_s['fox_logf'], st_s['conv'], st_s['moba_k'], st_s['moba_v'])
```

```python
import functools
import math

import jax
import jax.numpy as jnp
from jax import lax
from jax.experimental import pallas as pl
from jax.experimental.pallas import tpu as pltpu

F32 = jnp.float32
BF16 = jnp.bfloat16

HEAD_DIM = 128
CONV_WIDTH = 3
PAGE_SIZE = 128
MOBA_BLOCK = 256
MOBA_TOPK = 3
MOBA_GROUP = 4
CROSS_HEADS = 4
CROSS_HEAD_DIM = 256
ROPE_THETA = 10000.0
LN_EPS = 1e-5
ATTN_SCALE = HEAD_DIM ** -0.5
CROSS_SCALE = CROSS_HEAD_DIM ** -0.5

LANES = 128
BF16_SUBLANES = 16
VMEM_LIMIT_BYTES = 56 * 1024 * 1024
ROW_TILE_CAP = 704
COL_TILE_CAP = 512
FF_PAD = 1024
NEG = -1e30
NT_DIMS = (((1,), (1,)), ((), ()))


def _pick_tile(n, cap, mult):
    best = None
    for d in range(mult, min(n, cap) + 1, mult):
        if n % d == 0:
            best = d
    if best is None:
        raise ValueError(f"no tile for {n} (cap {cap}, multiple of {mult})")
    return best


def _params(*sem):
    return pltpu.CompilerParams(dimension_semantics=sem, vmem_limit_bytes=VMEM_LIMIT_BYTES)


def _normalize(z, st, g, b):
    return (z - st[:, 0:1]) * st[:, 1:2] * g + b


def _split3(x):
    hi = x.astype(BF16)
    r = x - hi.astype(F32)
    mid = r.astype(BF16)
    lo = (r - mid.astype(F32)).astype(BF16)
    return hi, mid, lo


def _dot01(a01, x, dims=(((1,), (0,)), ((), ()))):
    hi, mid, lo = _split3(x)
    acc = lax.dot_general(a01, hi, dims, preferred_element_type=F32)
    acc = acc + lax.dot_general(a01, mid, dims, preferred_element_type=F32)
    return acc + lax.dot_general(a01, lo, dims, preferred_element_type=F32)


def _proj_kernel(z_ref, st_ref, g_ref, b_ref, *rest, n_w):
    w_refs, o_ref, xb_ref = rest[:n_w], rest[n_w], rest[n_w + 1]

    @pl.when(pl.program_id(1) == 0)
    def _():
        xb_ref[...] = _normalize(z_ref[...], st_ref[...], g_ref[...], b_ref[...]).astype(BF16)

    xb = xb_ref[...]
    a = jnp.dot(xb, w_refs[0][...], preferred_element_type=F32)
    if n_w == 2:
        c = jnp.dot(xb, w_refs[1][...], preferred_element_type=F32)
        a = a * (1.0 / (1.0 + jnp.exp(-a))) * c
    o_ref[...] = a.astype(o_ref.dtype)


def _proj(act, ws, out_dtype, name, tn_cap=COL_TILE_CAP):
    z, st, g, b = act
    m, k = z.shape
    n = ws[0].shape[1]
    tm = _pick_tile(m, ROW_TILE_CAP, BF16_SUBLANES)
    tn = _pick_tile(n, tn_cap, LANES)
    row = lambda i, j: (i, 0)
    return pl.pallas_call(
        functools.partial(_proj_kernel, n_w=len(ws)),
        grid=(m // tm, n // tn),
        in_specs=[pl.BlockSpec((tm, k), row), pl.BlockSpec((tm, LANES), row),
                  pl.BlockSpec((1, k), lambda i, j: (0, 0)), pl.BlockSpec((1, k), lambda i, j: (0, 0))]
                 + [pl.BlockSpec((k, tn), lambda i, j: (0, j)) for _ in ws],
        out_specs=pl.BlockSpec((tm, tn), lambda i, j: (i, j)),
        out_shape=jax.ShapeDtypeStruct((m, n), out_dtype),
        scratch_shapes=[pltpu.VMEM((tm, k), BF16)],
        compiler_params=_params("parallel", "arbitrary"),
        name=name,
    )(z, st, g, b, *ws)


def _resid_kernel(a_ref, w_ref, zr_ref, sr_ref, gr_ref, br_ref, zo_ref, so_ref, mean_ref, m2_ref,
                  *, alpha, scale, n_total):
    j = pl.program_id(1)
    y = jnp.dot(a_ref[...], w_ref[...], preferred_element_type=F32)
    x_res = _normalize(zr_ref[...], sr_ref[...], gr_ref[...], br_ref[...])
    z_new = alpha * x_res + scale * y
    zo_ref[...] = z_new
    tn = z_new.shape[1]
    mu_t = jnp.mean(z_new, axis=1, keepdims=True)
    d = z_new - mu_t
    m2_t = jnp.sum(d * d, axis=1, keepdims=True)

    @pl.when(j == 0)
    def _():
        mean_ref[...] = mu_t
        m2_ref[...] = m2_t

    @pl.when(j > 0)
    def _():
        jf = jnp.zeros_like(mu_t) + j.astype(F32)
        delta = mu_t - mean_ref[...]
        mean_ref[...] = mean_ref[...] + delta / (jf + 1.0)
        m2_ref[...] = m2_ref[...] + m2_t + delta * delta * (tn * jf / (jf + 1.0))

    @pl.when(j == pl.num_programs(1) - 1)
    def _():
        rstd = lax.rsqrt(m2_ref[...] / n_total + LN_EPS)
        lane = lax.broadcasted_iota(jnp.int32, so_ref.shape, 1)
        so_ref[...] = jnp.where(lane == 0, mean_ref[...], jnp.where(lane == 1, rstd, 0.0))


def _resid(a, w, act, ln_g, ln_b, alpha, scale, name, tn_cap=COL_TILE_CAP):
    z, st, g, b = act
    m, k = a.shape
    n = w.shape[1]
    tm = _pick_tile(m, ROW_TILE_CAP, BF16_SUBLANES)
    tn = _pick_tile(n, tn_cap, LANES)
    z_new, st_new = pl.pallas_call(
        functools.partial(_resid_kernel, alpha=alpha, scale=scale, n_total=n),
        grid=(m // tm, n // tn),
        in_specs=[pl.BlockSpec((tm, k), lambda i, j: (i, 0)), pl.BlockSpec((k, tn), lambda i, j: (0, j)),
                  pl.BlockSpec((tm, tn), lambda i, j: (i, j)), pl.BlockSpec((tm, LANES), lambda i, j: (i, 0)),
                  pl.BlockSpec((1, tn), lambda i, j: (0, j)), pl.BlockSpec((1, tn), lambda i, j: (0, j))],
        out_specs=[pl.BlockSpec((tm, tn), lambda i, j: (i, j)), pl.BlockSpec((tm, LANES), lambda i, j: (i, 0))],
        out_shape=[jax.ShapeDtypeStruct((m, n), F32), jax.ShapeDtypeStruct((m, LANES), F32)],
        scratch_shapes=[pltpu.VMEM((tm, 1), F32), pltpu.VMEM((tm, 1), F32)],
        compiler_params=_params("parallel", "arbitrary"),
        name=name,
    )(a, w, z, st, g, b)
    return z_new, st_new, ln_g, ln_b


def _final_norm_kernel(z_ref, st_ref, g_ref, b_ref, o_ref):
    o_ref[...] = _normalize(z_ref[...], st_ref[...], g_ref[...], b_ref[...])


def _final_norm(act, row0, rows, name):
    z, st, g, b = act
    d = z.shape[1]
    tm = _pick_tile(math.gcd(rows, row0) if row0 else rows, 512, 8)
    off = row0 // tm
    return pl.pallas_call(
        _final_norm_kernel,
        grid=(rows // tm,),
        in_specs=[pl.BlockSpec((tm, d), lambda i: (i + off, 0)), pl.BlockSpec((tm, LANES), lambda i: (i + off, 0)),
                  pl.BlockSpec((1, d), lambda i: (0, 0)), pl.BlockSpec((1, d), lambda i: (0, 0))],
        out_specs=pl.BlockSpec((tm, d), lambda i: (i, 0)),
        out_shape=jax.ShapeDtypeStruct((rows, d), F32),
        compiler_params=_params("parallel"),
        name=name,
    )(z, st, g, b)


def _conv_kernel(ain_ref, ab_ref, ac_ref, buf_ref, w_ref, o_ref, nb_ref):
    u = ac_ref[...] * ain_ref[...]
    t = u.shape[0]
    row = lax.broadcasted_iota(jnp.int32, u.shape, 0)
    b0 = buf_ref[0, 0:1, :]
    b1 = buf_ref[0, 1:2, :]
    u1 = jnp.where(row == 0, b1, pltpu.roll(u, 1, 0))
    u2 = jnp.where(row == 0, b0, jnp.where(row == 1, b1, pltpu.roll(u, 2, 0)))
    y = w_ref[0:1, :] * u2 + w_ref[1:2, :] * u1 + w_ref[2:3, :] * u
    o_ref[...] = (ab_ref[...] * y).astype(o_ref.dtype)
    nb_ref[0] = u[t - (CONV_WIDTH - 1):, :]


def _conv(p, buf, w, row0, n_batch, t, ch, out_dtype, name):
    tc = _pick_tile(ch, 512, LANES)
    nc = ch // tc
    r0 = row0 // t
    return pl.pallas_call(
        _conv_kernel,
        grid=(n_batch, nc),
        in_specs=[pl.BlockSpec((t, tc), lambda b, c: (b + r0, c)),
                  pl.BlockSpec((t, tc), lambda b, c: (b + r0, c + nc)),
                  pl.BlockSpec((t, tc), lambda b, c: (b + r0, c + 2 * nc)),
                  pl.BlockSpec((1, CONV_WIDTH - 1, tc), lambda b, c: (b, 0, c)),
                  pl.BlockSpec((CONV_WIDTH, tc), lambda b, c: (0, c))],
        out_specs=[pl.BlockSpec((t, tc), lambda b, c: (b, c)),
                   pl.BlockSpec((1, CONV_WIDTH - 1, tc), lambda b, c: (b, 0, c))],
        out_shape=[jax.ShapeDtypeStruct((n_batch * t, ch), out_dtype),
                   jax.ShapeDtypeStruct((n_batch, CONV_WIDTH - 1, ch), F32)],
        compiler_params=_params("parallel", "parallel"),
        name=name,
    )(p, p, p, buf, w)


def _logf_kernel(f_ref, bf_ref, lf_ref, nc_ref):
    x = f_ref[...] + bf_ref[...]
    lf = jnp.minimum(x, 0.0) - jnp.log1p(jnp.exp(-jnp.abs(x)))
    lf_ref[...] = lf
    row = lax.broadcasted_iota(jnp.int32, lf.shape, 0)
    cs = lf
    shift = 1
    while shift < lf.shape[0]:
        cs = cs + jnp.where(row >= shift, pltpu.roll(cs, shift, 0), 0.0)
        shift *= 2
    nc_ref[...] = -cs


def _logf(f, bias, row0, n_batch, t, name):
    r0 = row0 // t
    return pl.pallas_call(
        _logf_kernel,
        grid=(n_batch,),
        in_specs=[pl.BlockSpec((t, LANES), lambda b: (b + r0, 0)), pl.BlockSpec((1, LANES), lambda b: (0, 0))],
        out_specs=[pl.BlockSpec((t, LANES), lambda b: (b, 0)), pl.BlockSpec((t, LANES), lambda b: (b, 0))],
        out_shape=[jax.ShapeDtypeStruct((n_batch * t, LANES), F32)] * 2,
        compiler_params=_params("parallel"),
        name=name,
    )(f, bias)


def _fox_prompt_kernel(q_ref, k_ref, v_ref, nc_ref, o_ref, *, tq):
    h = pl.program_id(1)
    qi = pl.program_id(2)
    qb = q_ref[...].astype(BF16)

    def block(n):
        start = pl.multiple_of(n * tq, tq)
        kb = k_ref[pl.ds(start, tq), :].astype(BF16)
        vb = v_ref[pl.ds(start, tq), :].astype(BF16)
        s = lax.dot_general(qb, kb, NT_DIMS, preferred_element_type=F32) * ATTN_SCALE
        return s + nc_ref[0, h, pl.ds(n, 1), :], vb

    s, vb = block(qi)
    row = lax.broadcasted_iota(jnp.int32, s.shape, 0)
    col = lax.broadcasted_iota(jnp.int32, s.shape, 1)
    s = jnp.where(col <= row, s, NEG)
    m = jnp.max(s, axis=1, keepdims=True)
    p = jnp.exp(s - m)
    l = jnp.sum(p, axis=1, keepdims=True)
    acc = jnp.dot(p.astype(BF16), vb, preferred_element_type=F32)

    def body(n, carry):
        m, l, acc = carry
        s, vb = block(n)
        m_new = jnp.maximum(m, jnp.max(s, axis=1, keepdims=True))
        corr = jnp.exp(m - m_new)
        p = jnp.exp(s - m_new)
        l = l * corr + jnp.sum(p, axis=1, keepdims=True)
        acc = acc * corr + jnp.dot(p.astype(BF16), vb, preferred_element_type=F32)
        return m_new, l, acc

    m, l, acc = lax.fori_loop(0, qi, body, (m, l, acc))
    o_ref[...] = (acc / l).astype(o_ref.dtype)


def _fox_prompt(p, nct, n_batch, t, heads, col_q, name):
    tq = nct.shape[3]
    cq = col_q // HEAD_DIM
    return pl.pallas_call(
        functools.partial(_fox_prompt_kernel, tq=tq),
        grid=(n_batch, heads, t // tq),
        in_specs=[pl.BlockSpec((tq, HEAD_DIM), lambda b, h, i: (b * (t // tq) + i, cq + h)),
                  pl.BlockSpec((t, HEAD_DIM), lambda b, h, i: (b, cq + heads + h)),
                  pl.BlockSpec((t, HEAD_DIM), lambda b, h, i: (b, cq + 2 * heads + h)),
                  pl.BlockSpec((1, heads, t // tq, tq), lambda b, h, i: (b, 0, 0, 0))],
        out_specs=pl.BlockSpec((tq, HEAD_DIM), lambda b, h, i: (b * (t // tq) + i, h)),
        out_shape=jax.ShapeDtypeStruct((n_batch * t, heads * HEAD_DIM), BF16),
        compiler_params=_params("parallel", "parallel", "arbitrary"),
        name=name,
    )(p, p, p, nct)


def _extract_diag(z, rows_per_head, n_heads):
    head = lax.broadcasted_iota(jnp.int32, (z.shape[0], HEAD_DIM), 0) // rows_per_head
    out = jnp.where(head == 0, z[:, 0:HEAD_DIM], 0.0)
    for hh in range(1, n_heads):
        out = out + jnp.where(head == hh, z[:, hh * HEAD_DIM:(hh + 1) * HEAD_DIM], 0.0)
    return out


def _softmax_step(s, v_bf16, m_ref, l_ref, acc_ref, rows_per_head, n_heads, first):
    if first:
        m_new = jnp.max(s, axis=1, keepdims=True)
    else:
        m_new = jnp.maximum(m_ref[...], jnp.max(s, axis=1, keepdims=True))
    p = jnp.exp(s - m_new)
    pv = _extract_diag(jnp.dot(p.astype(BF16), v_bf16, preferred_element_type=F32), rows_per_head, n_heads)
    if first:
        l_ref[...] = jnp.sum(p, axis=1, keepdims=True)
        acc_ref[...] = pv
    else:
        corr = jnp.exp(m_ref[...] - m_new)
        l_ref[...] = l_ref[...] * corr + jnp.sum(p, axis=1, keepdims=True)
        acc_ref[...] = acc_ref[...] * corr + pv
    m_ref[...] = m_new


def _new_token_scores(qbd, kn_ref, n_new, rows_q):
    s = lax.dot_general(qbd, kn_ref[0].astype(BF16), NT_DIMS, preferred_element_type=F32) * ATTN_SCALE
    tq = lax.broadcasted_iota(jnp.int32, s.shape, 0) % rows_q
    col = lax.broadcasted_iota(jnp.int32, s.shape, 1)
    return s, (col <= tq) & (col < n_new)


def _fox_sample_kernel(pt_ref, qbd_ref, k_ref, v_ref, lf_ref, kn_ref, vn_ref, bn_ref, o_ref,
                       m_ref, l_ref, acc_ref, carry_ref, *, heads, n_new):
    del pt_ref
    pg = pl.program_id(1)
    qbd = qbd_ref[0]
    rows = qbd.shape[0]

    @pl.when(pg == 0)
    def _():
        s, ok = _new_token_scores(qbd, kn_ref, n_new, n_new)
        s = jnp.where(ok, s + bn_ref[0], NEG)
        _softmax_step(s, vn_ref[0].astype(BF16), m_ref, l_ref, acc_ref, n_new, heads, first=True)
        carry_ref[...] = jnp.zeros_like(carry_ref)

    lf = lf_ref[0]
    r = lax.broadcasted_iota(jnp.int32, (rows, heads), 0) // n_new
    c = lax.broadcasted_iota(jnp.int32, (rows, heads), 1)
    expand = (r == c).astype(BF16)
    lf_rows = _dot01(expand, lf, NT_DIMS)
    a = lax.broadcasted_iota(jnp.int32, (PAGE_SIZE, PAGE_SIZE), 0)
    bcol = lax.broadcasted_iota(jnp.int32, (PAGE_SIZE, PAGE_SIZE), 1)
    later = (a > bcol).astype(BF16)
    suffix = _dot01_right(lf_rows, later) + carry_ref[...]
    carry_ref[...] = carry_ref[...] + jnp.sum(lf_rows, axis=1, keepdims=True)
    s = lax.dot_general(qbd, k_ref[0].astype(BF16), NT_DIMS, preferred_element_type=F32) * ATTN_SCALE + suffix
    _softmax_step(s, v_ref[0].astype(BF16), m_ref, l_ref, acc_ref, n_new, heads, first=False)

    @pl.when(pg == pl.num_programs(1) - 1)
    def _():
        o_ref[0] = acc_ref[...] / l_ref[...]


def _dot01_right(x, b01):
    hi, mid, lo = _split3(x)
    acc = jnp.dot(hi, b01, preferred_element_type=F32)
    acc = acc + jnp.dot(mid, b01, preferred_element_type=F32)
    return acc + jnp.dot(lo, b01, preferred_element_type=F32)


def _fox_sample(page_table, qbd, k_pool, v_pool, lf_pool, k_new, v_new, bias_new, heads, n_new, name):
    n_batch, n_pages = page_table.shape
    width = heads * HEAD_DIM
    rows = heads * n_new
    pad = k_new.shape[1]
    page = lambda b, p, pt: (pt[b * n_pages + n_pages - 1 - p], 0, 0)
    per_b = lambda b, p, pt: (b, 0, 0)
    grid_spec = pltpu.PrefetchScalarGridSpec(
        num_scalar_prefetch=1,
        grid=(n_batch, n_pages),
        in_specs=[pl.BlockSpec((1, rows, width), per_b),
                  pl.BlockSpec((1, PAGE_SIZE, width), page), pl.BlockSpec((1, PAGE_SIZE, width), page),
                  pl.BlockSpec((1, PAGE_SIZE, heads), page),
                  pl.BlockSpec((1, pad, width), per_b), pl.BlockSpec((1, pad, width), per_b),
                  pl.BlockSpec((1, rows, pad), per_b)],
        out_specs=pl.BlockSpec((1, rows, HEAD_DIM), per_b),
        scratch_shapes=[pltpu.VMEM((rows, 1), F32), pltpu.VMEM((rows, 1), F32),
                        pltpu.VMEM((rows, HEAD_DIM), F32), pltpu.VMEM((rows, 1), F32)],
    )
    return pl.pallas_call(
        functools.partial(_fox_sample_kernel, heads=heads, n_new=n_new),
        grid_spec=grid_spec,
        out_shape=jax.ShapeDtypeStruct((n_batch, rows, HEAD_DIM), F32),
        compiler_params=_params("parallel", "arbitrary"),
        name=name,
    )(page_table.reshape(-1), qbd, k_pool, v_pool, lf_pool, k_new, v_new, bias_new)


def _rope_kernel(x_ref, c_ref, s_ref, o_ref):
    for hh in range(x_ref.shape[1] // HEAD_DIM):
        sl = slice(hh * HEAD_DIM, (hh + 1) * HEAD_DIM)
        x = x_ref[:, sl]
        o_ref[:, sl] = x * c_ref[...] + pltpu.roll(x, HEAD_DIM // 2, 1) * s_ref[...]


def _rope(p, cos2, sin2, width, name):
    m = p.shape[0]
    tm = _pick_tile(m, ROW_TILE_CAP, 8)
    tc = _pick_tile(width, 512, HEAD_DIM)
    return pl.pallas_call(
        _rope_kernel,
        grid=(m // tm, width // tc),
        in_specs=[pl.BlockSpec((tm, tc), lambda i, j: (i, j)),
                  pl.BlockSpec((tm, HEAD_DIM), lambda i, j: (i, 0)), pl.BlockSpec((tm, HEAD_DIM), lambda i, j: (i, 0))],
        out_specs=pl.BlockSpec((tm, tc), lambda i, j: (i, j)),
        out_shape=jax.ShapeDtypeStruct((m, width), F32),
        compiler_params=_params("parallel", "arbitrary"),
        name=name,
    )(p, cos2, sin2)


def _topk_select(gates, valid, topk):
    masked = [jnp.where(v, g, -jnp.inf) for g, v in zip(gates, valid)]
    sel = []
    for n, gn in enumerate(masked):
        cnt = jnp.zeros_like(gn)
        for mth, gm in enumerate(masked):
            if mth == n:
                continue
            ahead = (gm > gn) | (gm == gn) if mth < n else (gm > gn)
            cnt = cnt + jnp.where(ahead, 1.0, 0.0)
        sel.append(jnp.where(valid[n], jnp.where(cnt < topk, 1.0, 0.0), 0.0))
    return sel


def _moba_prompt_kernel(q_ref, k_ref, v_ref, o_ref, m_ref, l_ref, acc_ref, *, blk):
    qi = pl.program_id(2)
    q = q_ref[...]
    nb = k_ref.shape[0] // blk
    group = q.shape[1] // HEAD_DIM
    heads = [q[:, r * HEAD_DIM:(r + 1) * HEAD_DIM] for r in range(group)]
    q_sum = heads[0]
    for r in range(1, group):
        q_sum = q_sum + heads[r]
    gates = []
    for n in range(nb - 1):
        k_mean = jnp.mean(k_ref[n * blk:(n + 1) * blk, :], axis=0, keepdims=True)
        gates.append(jnp.sum(q_sum * k_mean, axis=1, keepdims=True))
    sel = _topk_select(gates, [n < qi for n in range(nb - 1)], MOBA_TOPK)
    q4 = jnp.concatenate(heads, axis=0).astype(BF16)

    def scores(start):
        kb = k_ref[pl.ds(start, blk), :].astype(BF16)
        vb = v_ref[pl.ds(start, blk), :].astype(BF16)
        return lax.dot_general(q4, kb, NT_DIMS, preferred_element_type=F32) * ATTN_SCALE, vb

    def update(s, vb, first):
        m_new = jnp.max(s, axis=1, keepdims=True)
        if not first:
            m_new = jnp.maximum(m_ref[...], m_new)
        p = jnp.exp(s - m_new)
        pv = jnp.dot(p.astype(BF16), vb, preferred_element_type=F32)
        if first:
            l_ref[...] = jnp.sum(p, axis=1, keepdims=True)
            acc_ref[...] = pv
        else:
            corr = jnp.exp(m_ref[...] - m_new)
            l_ref[...] = l_ref[...] * corr + jnp.sum(p, axis=1, keepdims=True)
            acc_ref[...] = acc_ref[...] * corr + pv
        m_ref[...] = m_new

    s, vb = scores(pl.multiple_of(qi * blk, blk))
    tok = lax.broadcasted_iota(jnp.int32, s.shape, 0) % blk
    col = lax.broadcasted_iota(jnp.int32, s.shape, 1)
    update(jnp.where(col <= tok, s, NEG), vb, first=True)

    for n in range(nb - 1):
        @pl.when(n < qi)
        def _(n=n):
            s, vb = scores(n * blk)
            keep = jnp.concatenate([sel[n]] * group, axis=0) > 0.5
            update(jnp.where(keep, s, NEG), vb, first=False)

    out = acc_ref[...] / l_ref[...]
    for r in range(group):
        o_ref[:, r * HEAD_DIM:(r + 1) * HEAD_DIM] = out[r * blk:(r + 1) * blk, :].astype(o_ref.dtype)


def _moba_prompt(qk, p, n_batch, t, kv_heads, col_v, name):
    blk = MOBA_BLOCK
    gw = MOBA_GROUP * HEAD_DIM
    ck = kv_heads * MOBA_GROUP
    cv = col_v // HEAD_DIM
    return pl.pallas_call(
        functools.partial(_moba_prompt_kernel, blk=blk),
        grid=(n_batch, kv_heads, t // blk),
        in_specs=[pl.BlockSpec((blk, gw), lambda b, g, i: (b * (t // blk) + i, g)),
                  pl.BlockSpec((t, HEAD_DIM), lambda b, g, i: (b, ck + g)),
                  pl.BlockSpec((t, HEAD_DIM), lambda b, g, i: (b, cv + g))],
        out_specs=pl.BlockSpec((blk, gw), lambda b, g, i: (b * (t // blk) + i, g)),
        out_shape=jax.ShapeDtypeStruct((n_batch * t, kv_heads * gw), BF16),
        scratch_shapes=[pltpu.VMEM((MOBA_GROUP * blk, 1), F32), pltpu.VMEM((MOBA_GROUP * blk, 1), F32),
                        pltpu.VMEM((MOBA_GROUP * blk, HEAD_DIM), F32)],
        compiler_params=_params("parallel", "parallel", "arbitrary"),
        name=name,
    )(qk, qk, p)


def _kmean_kernel(pt_ref, *refs, pages_per_block):
    del pt_ref
    o_ref = refs[pages_per_block]
    tot = jnp.sum(refs[0][0], axis=0, keepdims=True)
    for r in refs[1:pages_per_block]:
        tot = tot + jnp.sum(r[0], axis=0, keepdims=True)
    o_ref[0, pl.ds(pl.program_id(1), 1), :] = tot / (pages_per_block * PAGE_SIZE)


def _moba_kmean(page_table, k_pool, name):
    n_batch, n_pages = page_table.shape
    ppb = MOBA_BLOCK // PAGE_SIZE
    nb = n_pages // ppb
    width = k_pool.shape[2]

    def page(which):
        return lambda b, n, pt: (pt[b * n_pages + n * ppb + which], 0, 0)

    grid_spec = pltpu.PrefetchScalarGridSpec(
        num_scalar_prefetch=1,
        grid=(n_batch, nb),
        in_specs=[pl.BlockSpec((1, PAGE_SIZE, width), page(w)) for w in range(ppb)],
        out_specs=pl.BlockSpec((1, nb, width), lambda b, n, pt: (b, 0, 0)),
    )
    return pl.pallas_call(
        functools.partial(_kmean_kernel, pages_per_block=ppb),
        grid_spec=grid_spec,
        out_shape=jax.ShapeDtypeStruct((n_batch, nb, width), F32),
        compiler_params=_params("parallel", "arbitrary"),
        name=name,
    )(page_table.reshape(-1), *([k_pool] * ppb))


def _moba_sample_kernel(pt_ref, q_ref, qbd_ref, km_ref, k_ref, v_ref, kn_ref, vn_ref, o_ref,
                        m_ref, l_ref, acc_ref, sel_ref, *, kv_heads, n_new):
    del pt_ref
    pg = pl.program_id(1)
    qbd = qbd_ref[0]
    rows_per_head = MOBA_GROUP * n_new
    nb = km_ref.shape[1]

    @pl.when(pg == 0)
    def _():
        q = q_ref[0]
        lane = lax.broadcasted_iota(jnp.int32, (n_new, nb), 1)
        for g in range(kv_heads):
            q_sum = q[:, g * MOBA_GROUP * HEAD_DIM:(g * MOBA_GROUP + 1) * HEAD_DIM]
            for r in range(1, MOBA_GROUP):
                c0 = (g * MOBA_GROUP + r) * HEAD_DIM
                q_sum = q_sum + q[:, c0:c0 + HEAD_DIM]
            k_mean = km_ref[0, :, g * HEAD_DIM:(g + 1) * HEAD_DIM]
            gate = _dot_f32_nt(q_sum, k_mean)
            cnt = jnp.zeros_like(gate)
            for mth in range(nb):
                gm = gate[:, mth:mth + 1]
                ahead = (gm > gate) | ((gm == gate) & (mth < lane))
                cnt = cnt + jnp.where(ahead, 1.0, 0.0)
            sel = jnp.where(cnt < MOBA_TOPK, 1.0, 0.0)
            for r in range(MOBA_GROUP):
                r0 = g * rows_per_head + r * n_new
                sel_ref[r0:r0 + n_new, 0:nb] = sel
        s, ok = _new_token_scores(qbd, kn_ref, n_new, n_new)
        _softmax_step(jnp.where(ok, s, NEG), vn_ref[0].astype(BF16), m_ref, l_ref, acc_ref,
                      rows_per_head, kv_heads, first=True)

    blk = pg // (MOBA_BLOCK // PAGE_SIZE)
    lane = lax.broadcasted_iota(jnp.int32, (qbd.shape[0], nb), 1)
    keep = jnp.sum(jnp.where(lane == blk, sel_ref[:, 0:nb], 0.0), axis=1, keepdims=True) > 0.5
    s = lax.dot_general(qbd, k_ref[0].astype(BF16), NT_DIMS, preferred_element_type=F32) * ATTN_SCALE
    _softmax_step(jnp.where(keep, s, NEG), v_ref[0].astype(BF16), m_ref, l_ref, acc_ref,
                  rows_per_head, kv_heads, first=False)

    @pl.when(pg == pl.num_programs(1) - 1)
    def _():
        o_ref[0] = acc_ref[...] / l_ref[...]


def _dot_f32_nt(a, b):
    a3, b3 = _split3(a), _split3(b)
    acc = None
    for ia, ap in enumerate(a3):
        for ib, bp in enumerate(b3):
            if ia + ib > 2:
                continue
            term = lax.dot_general(ap, bp, NT_DIMS, preferred_element_type=F32)
            acc = term if acc is None else acc + term
    return acc


def _moba_sample(page_table, q, qbd, k_mean, k_pool, v_pool, k_new, v_new, kv_heads, n_new, name):
    n_batch, n_pages = page_table.shape
    width = kv_heads * HEAD_DIM
    rows = kv_heads * MOBA_GROUP * n_new
    pad = k_new.shape[1]
    nb = k_mean.shape[1]
    page = lambda b, p, pt: (pt[b * n_pages + p], 0, 0)
    per_b = lambda b, p, pt: (b, 0, 0)
    grid_spec = pltpu.PrefetchScalarGridSpec(
        num_scalar_prefetch=1,
        grid=(n_batch, n_pages),
        in_specs=[pl.BlockSpec((1, n_new, q.shape[2]), per_b), pl.BlockSpec((1, rows, width), per_b),
                  pl.BlockSpec((1, nb, width), per_b),
                  pl.BlockSpec((1, PAGE_SIZE, width), page), pl.BlockSpec((1, PAGE_SIZE, width), page),
                  pl.BlockSpec((1, pad, width), per_b), pl.BlockSpec((1, pad, width), per_b)],
        out_specs=pl.BlockSpec((1, rows, HEAD_DIM), per_b),
        scratch_shapes=[pltpu.VMEM((rows, 1), F32), pltpu.VMEM((rows, 1), F32),
                        pltpu.VMEM((rows, HEAD_DIM), F32), pltpu.VMEM((rows, LANES), F32)],
    )
    return pl.pallas_call(
        functools.partial(_moba_sample_kernel, kv_heads=kv_heads, n_new=n_new),
        grid_spec=grid_spec,
        out_shape=jax.ShapeDtypeStruct((n_batch, rows, HEAD_DIM), F32),
        compiler_params=_params("parallel", "arbitrary"),
        name=name,
    )(page_table.reshape(-1), q, qbd, k_mean, k_pool, v_pool, k_new, v_new)


def _cross_kernel(q_ref, k_ref, v_ref, o_ref):
    s = lax.dot_general(q_ref[...].astype(BF16), k_ref[...].astype(BF16), NT_DIMS, preferred_element_type=F32) * CROSS_SCALE
    m = jnp.max(s, axis=1, keepdims=True)
    p = jnp.exp(s - m)
    l = jnp.sum(p, axis=1, keepdims=True)
    o = jnp.dot(p.astype(BF16), v_ref[...].astype(BF16), preferred_element_type=F32)
    o_ref[...] = (o / l).astype(o_ref.dtype)


def _cross(q, mem_k, mem_v, row0, n_batch, t, out_dtype, name):
    mem = mem_k.shape[0] // n_batch
    tq = _pick_tile(t, 512, 8)
    nq = t // tq
    r0 = row0 // tq
    return pl.pallas_call(
        _cross_kernel,
        grid=(n_batch, nq, CROSS_HEADS),
        in_specs=[pl.BlockSpec((tq, CROSS_HEAD_DIM), lambda b, i, h: (r0 + b * nq + i, h)),
                  pl.BlockSpec((mem, CROSS_HEAD_DIM), lambda b, i, h: (b, h)),
                  pl.BlockSpec((mem, CROSS_HEAD_DIM), lambda b, i, h: (b, h))],
        out_specs=pl.BlockSpec((tq, CROSS_HEAD_DIM), lambda b, i, h: (b * nq + i, h)),
        out_shape=jax.ShapeDtypeStruct((n_batch * t, CROSS_HEADS * CROSS_HEAD_DIM), out_dtype),
        compiler_params=_params("parallel", "parallel", "parallel"),
        name=name,
    )(q, mem_k, mem_v)


def _block_diag_queries(q, n_heads, rows_per_head_of):
    nb, t, w = q.shape
    hq = w // HEAD_DIM
    qh = q.reshape(nb, t, hq, HEAD_DIM).transpose(0, 2, 1, 3)
    onehot = (jnp.arange(n_heads)[None, :] == jnp.asarray([rows_per_head_of(i) for i in range(hq)])[:, None])
    out = qh[:, :, :, None, :] * onehot.astype(q.dtype)[None, :, None, :, None]
    return out.reshape(nb, hq * t, n_heads * HEAD_DIM).astype(BF16)


def _pad_rows(x, rows):
    return jnp.pad(x, ((0, 0), (0, rows - x.shape[1]), (0, 0)))


def kernel(x_prompt, x_sample, cache_fox_k, cache_fox_v, cache_fox_logf, state_conv, cache_moba_k, cache_moba_v, cache_mem_k, cache_mem_v, page_table, mem_prompt, w_in_even, b_forget, w_conv, w_out_even, w_in_odd, w_out_odd, w_cross_q, w_cross_k, w_cross_v, w_cross_o, w_ffa_gate, w_ffa_up, w_ffa_down, w_ffb_gate, w_ffb_up, w_ffb_down, ln_g, ln_b):
    bp, t, d = x_prompt.shape
    bs, ts, _ = x_sample.shape
    depth = ln_g.shape[0]
    mp, ms = bp * t, bs * ts
    m = mp + ms
    alpha = (2 * depth) ** 0.25
    conv_ch = d // 2
    fox_heads = conv_ch // HEAD_DIM
    fox_w = fox_heads * HEAD_DIM
    moba_heads = d // HEAD_DIM
    kv_heads = moba_heads // MOBA_GROUP
    kv_w = kv_heads * HEAD_DIM
    cross_w = CROSS_HEADS * CROSS_HEAD_DIM
    mem_len = mem_prompt.shape[1]
    n_pages = page_table.shape[1]
    past_len = n_pages * PAGE_SIZE
    d_ff = w_ffa_gate.shape[2]
    ff_pad = -d_ff % FF_PAD
    new_pad = BF16_SUBLANES

    def ffn_weights(wg, wu, wd):
        return (jnp.pad(wg.astype(BF16), ((0, 0), (0, ff_pad))), jnp.pad(wu.astype(BF16), ((0, 0), (0, ff_pad))),
                jnp.pad(wd.astype(BF16), ((0, ff_pad), (0, 0))))

    def ffn(act, wts, g, b, name):
        wg, wu, wd = wts
        hidden = _proj(act, (wg, wu), BF16, name + "_swiglu")
        return _resid(hidden, wd, act, g, b, alpha, 0.5, name + "_down", tn_cap=256)

    pos = jnp.concatenate([jnp.tile(jnp.arange(t, dtype=jnp.int32), bp),
                           jnp.tile(past_len + jnp.arange(ts, dtype=jnp.int32), bs)])
    half = HEAD_DIM // 2
    inv_freq = ROPE_THETA ** (-jnp.arange(half, dtype=F32) / half)
    ang = pos.astype(F32)[:, None] * inv_freq[None, :]
    cos2 = jnp.concatenate([jnp.cos(ang), jnp.cos(ang)], axis=1)
    sin2 = jnp.concatenate([-jnp.sin(ang), jnp.sin(ang)], axis=1)

    ones = jnp.ones((1, d), F32)
    zeros = jnp.zeros((1, d), F32)
    unit_stats = lambda rows: jnp.zeros((rows, LANES), F32).at[:, 1].set(1.0)

    mem_act = (mem_prompt.reshape(bp * mem_len, d), unit_stats(bp * mem_len), ones, zeros)
    mem_k_p = jnp.stack([_proj(mem_act, (w_cross_k[i].astype(BF16),), F32, f"mem_k{i}") for i in range(depth)])
    mem_v_p = jnp.stack([_proj(mem_act, (w_cross_v[i].astype(BF16),), F32, f"mem_v{i}") for i in range(depth)])

    x0 = jnp.concatenate([x_prompt.reshape(mp, d), x_sample.reshape(ms, d)], axis=0)
    act = (x0, unit_stats(m), ones, zeros)
    st_p = {name: [] for name in ("fox_k", "fox_v", "fox_logf", "conv", "moba_k", "moba_v")}
    st_s = {name: [] for name in st_p}

    for i in range(depth):
        g = [ln_g[i, s].reshape(1, d) for s in range(4)]
        b = [ln_b[i, s].reshape(1, d) for s in range(4)]
        j = i // 2
        act = ffn(act, ffn_weights(w_ffa_gate[i], w_ffa_up[i], w_ffa_down[i]), g[0], b[0], f"ffa{i}")

        if i % 2 == 0:
            n_main = 3 * conv_ch + 3 * fox_w
            w_in = w_in_even[j]
            p = _proj(act, (w_in[:, :n_main].astype(BF16),), F32, f"in_even{i}")
            w_f = jnp.pad(w_in[:, n_main:].astype(BF16), ((0, 0), (0, LANES - fox_heads)))
            f = _proj(act, (w_f,), F32, f"in_forget{i}")
            b_f = jnp.pad(b_forget[j].astype(F32), (0, LANES - fox_heads)).reshape(1, LANES)
            col_q, col_k, col_v = 3 * conv_ch, 3 * conv_ch + fox_w, 3 * conv_ch + 2 * fox_w

            a_p, conv_p = _conv(p, jnp.zeros((bp, CONV_WIDTH - 1, conv_ch), F32), w_conv[j], 0, bp, t, conv_ch,
                                BF16, f"conv_p{i}")
            a_s, conv_s = _conv(p, state_conv[j], w_conv[j], mp, bs, ts, conv_ch, F32, f"conv_s{i}")
            a_s = a_s.astype(BF16)

            lf_p, nc_p = _logf(f, b_f, 0, bp, t, f"logf_p{i}")
            lf_s, nc_s = _logf(f, b_f, mp, bs, ts, f"logf_s{i}")
            tq = _pick_tile(t, 256, LANES)
            nct = nc_p[:, :fox_heads].reshape(bp, t, fox_heads).transpose(0, 2, 1).reshape(bp, fox_heads, t // tq, tq)
            o_p = _fox_prompt(p, nct, bp, t, fox_heads, col_q, f"fox_p{i}")

            ps = p[mp:]
            q_s = ps[:, col_q:col_q + fox_w].reshape(bs, ts, fox_w)
            k_s = ps[:, col_k:col_k + fox_w].reshape(bs, ts, fox_w)
            v_s = ps[:, col_v:col_v + fox_w].reshape(bs, ts, fox_w)
            qbd = _block_diag_queries(q_s, fox_heads, lambda hq: hq)
            ncs = nc_s[:, :fox_heads].reshape(bs, ts, fox_heads).transpose(0, 2, 1)
            bias_new = jnp.broadcast_to(ncs[:, :, None, :], (bs, fox_heads, ts, ts)).reshape(bs, fox_heads * ts, ts)
            bias_new = jnp.pad(bias_new, ((0, 0), (0, 0), (0, new_pad - ts)))
            n_phys = cache_fox_k.shape[1]
            o_s = _fox_sample(page_table, qbd,
                              cache_fox_k[j].reshape(n_phys, PAGE_SIZE, fox_w),
                              cache_fox_v[j].reshape(n_phys, PAGE_SIZE, fox_w),
                              cache_fox_logf[j], _pad_rows(k_s, new_pad), _pad_rows(v_s, new_pad), bias_new,
                              fox_heads, ts, f"fox_s{i}")
            o_s = o_s.reshape(bs, fox_heads, ts, HEAD_DIM).transpose(0, 2, 1, 3).reshape(ms, fox_w).astype(BF16)

            mix_in = jnp.concatenate([jnp.concatenate([a_p, o_p], axis=1), jnp.concatenate([a_s, o_s], axis=1)], axis=0)
            w_out = w_out_even[j].astype(BF16)

            st_p["fox_k"].append(p[:mp, col_k:col_k + fox_w].reshape(bp, t, fox_heads, HEAD_DIM))
            st_p["fox_v"].append(p[:mp, col_v:col_v + fox_w].reshape(bp, t, fox_heads, HEAD_DIM))
            st_p["fox_logf"].append(lf_p[:, :fox_heads].reshape(bp, t, fox_heads))
            st_p["conv"].append(conv_p)
            st_s["fox_k"].append(k_s.reshape(bs, ts, fox_heads, HEAD_DIM))
            st_s["fox_v"].append(v_s.reshape(bs, ts, fox_heads, HEAD_DIM))
            st_s["fox_logf"].append(lf_s[:, :fox_heads].reshape(bs, ts, fox_heads))
            st_s["conv"].append(conv_s)
        else:
            p = _proj(act, (w_in_odd[j].astype(BF16),), F32, f"in_odd{i}")
            qk_w = d + kv_w
            qk = _rope(p, cos2, sin2, qk_w, f"rope{i}")
            o_p = _moba_prompt(qk, p, bp, t, kv_heads, qk_w, f"moba_p{i}")

            q_s = qk[mp:, :d].reshape(bs, ts, d)
            k_s = qk[mp:, d:].reshape(bs, ts, kv_w)
            v_s = p[mp:, qk_w:].reshape(bs, ts, kv_w)
            qbd = _block_diag_queries(q_s, kv_heads, lambda hq: hq // MOBA_GROUP)
            n_phys = cache_moba_k.shape[1]
            k_pool = cache_moba_k[j].reshape(n_phys, PAGE_SIZE, kv_w)
            v_pool = cache_moba_v[j].reshape(n_phys, PAGE_SIZE, kv_w)
            k_mean = _moba_kmean(page_table, k_pool, f"moba_kmean{i}")
            o_s = _moba_sample(page_table, q_s, qbd, k_mean, k_pool, v_pool,
                               _pad_rows(k_s, new_pad), _pad_rows(v_s, new_pad), kv_heads, ts, f"moba_s{i}")
            o_s = o_s.reshape(bs, moba_heads, ts, HEAD_DIM).transpose(0, 2, 1, 3).reshape(ms, d).astype(BF16)

            mix_in = jnp.concatenate([o_p, o_s], axis=0)
            w_out = w_out_odd[j].astype(BF16)

            st_p["moba_k"].append(qk[:mp, d:].reshape(bp, t, kv_heads, HEAD_DIM))
            st_p["moba_v"].append(p[:mp, qk_w:].reshape(bp, t, kv_heads, HEAD_DIM))
            st_s["moba_k"].append(k_s.reshape(bs, ts, kv_heads, HEAD_DIM))
            st_s["moba_v"].append(v_s.reshape(bs, ts, kv_heads, HEAD_DIM))

        act = _resid(mix_in, w_out, act, g[1], b[1], alpha, 1.0, f"mix_out{i}")

        q = _proj(act, (w_cross_q[i].astype(BF16),), F32, f"cross_q{i}")
        o_p = _cross(q, mem_k_p[i], mem_v_p[i], 0, bp, t, BF16, f"cross_p{i}")
        o_s = _cross(q, cache_mem_k[i].reshape(bs * mem_len, cross_w), cache_mem_v[i].reshape(bs * mem_len, cross_w),
                     mp, bs, ts, F32, f"cross_s{i}").astype(BF16)
        act = _resid(jnp.concatenate([o_p, o_s], axis=0), w_cross_o[i].astype(BF16), act, g[2], b[2], alpha, 1.0,
                     f"cross_o{i}")

        act = ffn(act, ffn_weights(w_ffb_gate[i], w_ffb_up[i], w_ffb_down[i]), g[3], b[3], f"ffb{i}")

    y_prompt = _final_norm(act, 0, mp, "y_prompt").reshape(bp, t, d)
    y_sample = _final_norm(act, mp, ms, "y_sample").reshape(bs, ts, d)
    stack = lambda xs: jnp.stack(xs)
    mem_shape = (depth, bp, mem_len, CROSS_HEADS, CROSS_HEAD_DIM)
    return (y_prompt, y_sample,
            stack(st_p["fox_k"]), stack(st_p["fox_v"]), stack(st_p["fox_logf"]), stack(st_p["conv"]),
            stack(st_p["moba_k"]), stack(st_p["moba_v"]),
            mem_k_p.reshape(mem_shape), mem_v_p.reshape(mem_shape),
            stack(st_s["fox_k"]), stack(st_s["fox_v"]), stack(st_s["fox_logf"]), stack(st_s["conv"]),
            stack(st_s["moba_k"]), stack(st_s["moba_v"]))
```

```python
import functools
import math

import jax
import jax.numpy as jnp
from jax import lax
from jax.experimental import pallas as pl
from jax.experimental.pallas import tpu as pltpu

F32 = jnp.float32
BF16 = jnp.bfloat16

HEAD_DIM = 128
CONV_WIDTH = 3
PAGE_SIZE = 128
MOBA_BLOCK = 256
MOBA_TOPK = 3
MOBA_GROUP = 4
CROSS_HEADS = 4
CROSS_HEAD_DIM = 256
ROPE_THETA = 10000.0
LN_EPS = 1e-5
ATTN_SCALE = HEAD_DIM ** -0.5
CROSS_SCALE = CROSS_HEAD_DIM ** -0.5

LANES = 128
BF16_SUBLANES = 16
VMEM_LIMIT_BYTES = 56 * 1024 * 1024
ROW_TILE_CAP = 704
COL_TILE_CAP = 512
FOX_PAGES_PER_STEP = 2
MOBA_PAGES_PER_STEP = 4
NEG = -1e30
NT_DIMS = (((1,), (1,)), ((), ()))


def _pick_tile(n, cap, mult):
    best = None
    for d in range(mult, min(n, cap) + 1, mult):
        if n % d == 0:
            best = d
    if best is None:
        raise ValueError(f"no tile for {n} (cap {cap}, multiple of {mult})")
    return best


def _params(*sem):
    return pltpu.CompilerParams(dimension_semantics=sem, vmem_limit_bytes=VMEM_LIMIT_BYTES)


def _normalize(z, st, g, b):
    return (z - st[:, 0:1]) * st[:, 1:2] * g + b


def _split3(x):
    hi = x.astype(BF16)
    r = x - hi.astype(F32)
    mid = r.astype(BF16)
    lo = (r - mid.astype(F32)).astype(BF16)
    return hi, mid, lo


def _dot01(a01, x, dims=(((1,), (0,)), ((), ()))):
    hi, mid, lo = _split3(x)
    acc = lax.dot_general(a01, hi, dims, preferred_element_type=F32)
    acc = acc + lax.dot_general(a01, mid, dims, preferred_element_type=F32)
    return acc + lax.dot_general(a01, lo, dims, preferred_element_type=F32)


def _proj_kernel(z_ref, st_ref, g_ref, b_ref, *rest, n_w):
    w_refs, o_ref, xb_ref = rest[:n_w], rest[n_w], rest[n_w + 1]

    @pl.when(pl.program_id(1) == 0)
    def _():
        xb_ref[...] = _normalize(z_ref[...], st_ref[...], g_ref[...], b_ref[...]).astype(BF16)

    xb = xb_ref[...]
    a = jnp.dot(xb, w_refs[0][...], preferred_element_type=F32)
    if n_w == 2:
        c = jnp.dot(xb, w_refs[1][...], preferred_element_type=F32)
        a = a * (1.0 / (1.0 + jnp.exp(-a))) * c
    o_ref[...] = a.astype(o_ref.dtype)


def _proj(act, ws, out_dtype, name, tn_cap=COL_TILE_CAP):
    z, st, g, b = act
    m, k = z.shape
    n = ws[0].shape[1]
    tm = _pick_tile(m, ROW_TILE_CAP, BF16_SUBLANES)
    tn = _pick_tile(n, tn_cap, LANES)
    row = lambda i, j: (i, 0)
    return pl.pallas_call(
        functools.partial(_proj_kernel, n_w=len(ws)),
        grid=(m // tm, n // tn),
        in_specs=[pl.BlockSpec((tm, k), row), pl.BlockSpec((tm, LANES), row),
                  pl.BlockSpec((1, k), lambda i, j: (0, 0)), pl.BlockSpec((1, k), lambda i, j: (0, 0))]
                 + [pl.BlockSpec((k, tn), lambda i, j: (0, j)) for _ in ws],
        out_specs=pl.BlockSpec((tm, tn), lambda i, j: (i, j)),
        out_shape=jax.ShapeDtypeStruct((m, n), out_dtype),
        scratch_shapes=[pltpu.VMEM((tm, k), BF16)],
        compiler_params=_params("parallel", "arbitrary"),
        name=name,
    )(z, st, g, b, *ws)


def _resid_kernel(a_ref, w_ref, zr_ref, sr_ref, gr_ref, br_ref, zo_ref, so_ref, mean_ref, m2_ref,
                  *, alpha, scale, n_total):
    j = pl.program_id(1)
    y = jnp.dot(a_ref[...], w_ref[...], preferred_element_type=F32)
    x_res = _normalize(zr_ref[...], sr_ref[...], gr_ref[...], br_ref[...])
    z_new = alpha * x_res + scale * y
    zo_ref[...] = z_new
    tn = z_new.shape[1]
    mu_t = jnp.mean(z_new, axis=1, keepdims=True)
    d = z_new - mu_t
    m2_t = jnp.sum(d * d, axis=1, keepdims=True)

    @pl.when(j == 0)
    def _():
        mean_ref[...] = mu_t
        m2_ref[...] = m2_t

    @pl.when(j > 0)
    def _():
        jf = jnp.zeros_like(mu_t) + j.astype(F32)
        delta = mu_t - mean_ref[...]
        mean_ref[...] = mean_ref[...] + delta / (jf + 1.0)
        m2_ref[...] = m2_ref[...] + m2_t + delta * delta * (tn * jf / (jf + 1.0))

    @pl.when(j == pl.num_programs(1) - 1)
    def _():
        rstd = lax.rsqrt(m2_ref[...] / n_total + LN_EPS)
        lane = lax.broadcasted_iota(jnp.int32, so_ref.shape, 1)
        so_ref[...] = jnp.where(lane == 0, mean_ref[...], jnp.where(lane == 1, rstd, 0.0))


def _resid(a, w, act, ln_g, ln_b, alpha, scale, name, tn_cap=COL_TILE_CAP):
    z, st, g, b = act
    m, k = a.shape
    n = w.shape[1]
    tm = _pick_tile(m, ROW_TILE_CAP, BF16_SUBLANES)
    tn = _pick_tile(n, tn_cap, LANES)
    z_new, st_new = pl.pallas_call(
        functools.partial(_resid_kernel, alpha=alpha, scale=scale, n_total=n),
        grid=(m // tm, n // tn),
        in_specs=[pl.BlockSpec((tm, k), lambda i, j: (i, 0)), pl.BlockSpec((k, tn), lambda i, j: (0, j)),
                  pl.BlockSpec((tm, tn), lambda i, j: (i, j)), pl.BlockSpec((tm, LANES), lambda i, j: (i, 0)),
                  pl.BlockSpec((1, tn), lambda i, j: (0, j)), pl.BlockSpec((1, tn), lambda i, j: (0, j))],
        out_specs=[pl.BlockSpec((tm, tn), lambda i, j: (i, j)), pl.BlockSpec((tm, LANES), lambda i, j: (i, 0))],
        out_shape=[jax.ShapeDtypeStruct((m, n), F32), jax.ShapeDtypeStruct((m, LANES), F32)],
        scratch_shapes=[pltpu.VMEM((tm, 1), F32), pltpu.VMEM((tm, 1), F32)],
        compiler_params=_params("parallel", "arbitrary"),
        name=name,
    )(a, w, z, st, g, b)
    return z_new, st_new, ln_g, ln_b


def _final_norm_kernel(z_ref, st_ref, g_ref, b_ref, o_ref):
    o_ref[...] = _normalize(z_ref[...], st_ref[...], g_ref[...], b_ref[...])


def _final_norm(act, row0, rows, name):
    z, st, g, b = act
    d = z.shape[1]
    tm = _pick_tile(math.gcd(rows, row0) if row0 else rows, 512, 8)
    off = row0 // tm
    return pl.pallas_call(
        _final_norm_kernel,
        grid=(rows // tm,),
        in_specs=[pl.BlockSpec((tm, d), lambda i: (i + off, 0)), pl.BlockSpec((tm, LANES), lambda i: (i + off, 0)),
                  pl.BlockSpec((1, d), lambda i: (0, 0)), pl.BlockSpec((1, d), lambda i: (0, 0))],
        out_specs=pl.BlockSpec((tm, d), lambda i: (i, 0)),
        out_shape=jax.ShapeDtypeStruct((rows, d), F32),
        compiler_params=_params("parallel"),
        name=name,
    )(z, st, g, b)


def _conv_kernel(ain_ref, ab_ref, ac_ref, buf_ref, w_ref, o_ref, nb_ref):
    u = ac_ref[...] * ain_ref[...]
    t = u.shape[0]
    row = lax.broadcasted_iota(jnp.int32, u.shape, 0)
    b0 = buf_ref[0, 0:1, :]
    b1 = buf_ref[0, 1:2, :]
    u1 = jnp.where(row == 0, b1, pltpu.roll(u, 1, 0))
    u2 = jnp.where(row == 0, b0, jnp.where(row == 1, b1, pltpu.roll(u, 2, 0)))
    y = w_ref[0:1, :] * u2 + w_ref[1:2, :] * u1 + w_ref[2:3, :] * u
    o_ref[...] = (ab_ref[...] * y).astype(o_ref.dtype)
    nb_ref[0] = u[t - (CONV_WIDTH - 1):, :]


def _conv(p, buf, w, row0, n_batch, t, ch, out_dtype, name):
    tc = _pick_tile(ch, 512, LANES)
    nc = ch // tc
    r0 = row0 // t
    return pl.pallas_call(
        _conv_kernel,
        grid=(n_batch, nc),
        in_specs=[pl.BlockSpec((t, tc), lambda b, c: (b + r0, c)),
                  pl.BlockSpec((t, tc), lambda b, c: (b + r0, c + nc)),
                  pl.BlockSpec((t, tc), lambda b, c: (b + r0, c + 2 * nc)),
                  pl.BlockSpec((1, CONV_WIDTH - 1, tc), lambda b, c: (b, 0, c)),
                  pl.BlockSpec((CONV_WIDTH, tc), lambda b, c: (0, c))],
        out_specs=[pl.BlockSpec((t, tc), lambda b, c: (b, c)),
                   pl.BlockSpec((1, CONV_WIDTH - 1, tc), lambda b, c: (b, 0, c))],
        out_shape=[jax.ShapeDtypeStruct((n_batch * t, ch), out_dtype),
                   jax.ShapeDtypeStruct((n_batch, CONV_WIDTH - 1, ch), F32)],
        compiler_params=_params("parallel", "parallel"),
        name=name,
    )(p, p, p, buf, w)


def _logf_kernel(f_ref, bf_ref, lf_ref, nc_ref):
    x = f_ref[...] + bf_ref[...]
    lf = jnp.minimum(x, 0.0) - jnp.log1p(jnp.exp(-jnp.abs(x)))
    lf_ref[...] = lf
    row = lax.broadcasted_iota(jnp.int32, lf.shape, 0)
    cs = lf
    shift = 1
    while shift < lf.shape[0]:
        cs = cs + jnp.where(row >= shift, pltpu.roll(cs, shift, 0), 0.0)
        shift *= 2
    nc_ref[...] = -cs


def _logf(f, bias, row0, n_batch, t, name):
    r0 = row0 // t
    return pl.pallas_call(
        _logf_kernel,
        grid=(n_batch,),
        in_specs=[pl.BlockSpec((t, LANES), lambda b: (b + r0, 0)), pl.BlockSpec((1, LANES), lambda b: (0, 0))],
        out_specs=[pl.BlockSpec((t, LANES), lambda b: (b, 0)), pl.BlockSpec((t, LANES), lambda b: (b, 0))],
        out_shape=[jax.ShapeDtypeStruct((n_batch * t, LANES), F32)] * 2,
        compiler_params=_params("parallel"),
        name=name,
    )(f, bias)


def _softmax_pv(s, vb):
    m = jnp.max(s, axis=1, keepdims=True)
    p = jnp.exp(s - m)
    l = jnp.sum(p, axis=1, keepdims=True)
    return jnp.dot(p.astype(BF16), vb, preferred_element_type=F32) / l


def _causal_tail(s, width):
    row = lax.broadcasted_iota(jnp.int32, (s.shape[0], width), 0)
    col = lax.broadcasted_iota(jnp.int32, (s.shape[0], width), 1)
    tail = jnp.where(col <= row, s[:, s.shape[1] - width:], NEG)
    if s.shape[1] == width:
        return tail
    return jnp.concatenate([s[:, :s.shape[1] - width], tail], axis=1)


def _fox_prompt_kernel(q_ref, k_ref, v_ref, nc_ref, o_ref, *, tq):
    h = pl.program_id(1)
    qi = pl.program_id(2)
    qb = q_ref[...].astype(BF16)
    for nk in range(1, k_ref.shape[0] // tq + 1):
        @pl.when(qi == nk - 1)
        def _(nk=nk):
            w = nk * tq
            kb = k_ref[0:w, :].astype(BF16)
            vb = v_ref[0:w, :].astype(BF16)
            s = lax.dot_general(qb, kb, NT_DIMS, preferred_element_type=F32) * ATTN_SCALE + nc_ref[0, h, :, 0:w]
            o_ref[...] = _softmax_pv(_causal_tail(s, tq), vb).astype(o_ref.dtype)


def _fox_prompt(p, nct, n_batch, t, heads, col_q, tq, name):
    cq = col_q // HEAD_DIM
    return pl.pallas_call(
        functools.partial(_fox_prompt_kernel, tq=tq),
        grid=(n_batch, heads, t // tq),
        in_specs=[pl.BlockSpec((tq, HEAD_DIM), lambda b, h, i: (b * (t // tq) + i, cq + h)),
                  pl.BlockSpec((t, HEAD_DIM), lambda b, h, i: (b, cq + heads + h)),
                  pl.BlockSpec((t, HEAD_DIM), lambda b, h, i: (b, cq + 2 * heads + h)),
                  pl.BlockSpec((1, heads, 1, t), lambda b, h, i: (b, 0, 0, 0))],
        out_specs=pl.BlockSpec((tq, HEAD_DIM), lambda b, h, i: (b * (t // tq) + i, h)),
        out_shape=jax.ShapeDtypeStruct((n_batch * t, heads * HEAD_DIM), BF16),
        compiler_params=_params("parallel", "parallel", "arbitrary"),
        name=name,
    )(p, p, p, nct)


def _online_update(s_parts, v_parts, m_ref, l_ref, acc_ref, first):
    m_new = jnp.max(s_parts[0], axis=1, keepdims=True)
    for s in s_parts[1:]:
        m_new = jnp.maximum(m_new, jnp.max(s, axis=1, keepdims=True))
    if not first:
        m_new = jnp.maximum(m_ref[...], m_new)
    l_new, pv = None, None
    for s, v in zip(s_parts, v_parts):
        p = jnp.exp(s - m_new)
        l_part = jnp.sum(p, axis=1, keepdims=True)
        pv_part = jnp.dot(p.astype(BF16), v, preferred_element_type=F32)
        l_new = l_part if l_new is None else l_new + l_part
        pv = pv_part if pv is None else pv + pv_part
    if first:
        l_ref[...] = l_new
        acc_ref[...] = pv
    else:
        corr = jnp.exp(m_ref[...] - m_new)
        l_ref[...] = l_ref[...] * corr + l_new
        acc_ref[...] = acc_ref[...] * corr + pv
    m_ref[...] = m_new


def _scores(q_rows, k_ref):
    return lax.dot_general(q_rows, k_ref[...].astype(BF16), NT_DIMS, preferred_element_type=F32) * ATTN_SCALE


def _lf_scan_kernel(lf_ref, suf_ref, tot_ref, *, heads):
    lf = lf_ref[...]
    w = lf.shape[1]
    lane = lax.broadcasted_iota(jnp.int32, lf.shape, 1)
    a = lf
    shift = heads
    while shift < w:
        a = a + jnp.where(lane + shift < w, pltpu.roll(a, w - shift, 1), 0.0)
        shift *= 2
    suf_ref[...] = jnp.where(lane + heads < w, pltpu.roll(a, w - heads, 1), 0.0)
    tot = jnp.where(lane < heads, a, 0.0)
    shift = heads
    while shift < w:
        tot = tot + pltpu.roll(tot, shift, 1)
        shift *= 2
    tot_ref[...] = tot


def _lf_scan(lf2, heads, name):
    n_phys, w = lf2.shape
    tp = _pick_tile(n_phys, 256, 8)
    spec = pl.BlockSpec((tp, w), lambda i: (i, 0))
    return pl.pallas_call(
        functools.partial(_lf_scan_kernel, heads=heads),
        grid=(n_phys // tp,),
        in_specs=[spec],
        out_specs=[spec, spec],
        out_shape=[jax.ShapeDtypeStruct((n_phys, w), F32)] * 2,
        compiler_params=_params("parallel"),
        name=name,
    )(lf2)


def _fox_sample_kernel(pt_ref, q_ref, hm_ref, nm_ref, kn_ref, vn_ref, bn_ref, *refs, pps):
    del pt_ref
    k_refs, v_refs = refs[:pps], refs[pps:2 * pps]
    suf_refs, tot_refs = refs[2 * pps:3 * pps], refs[3 * pps:4 * pps]
    o_ref, m_ref, l_ref, acc_ref, carry_ref = refs[4 * pps:]
    pg = pl.program_id(1)
    q_rows = q_ref[0]

    @pl.when(pg == 0)
    def _():
        s = _scores(q_rows, kn_ref.at[0]) + bn_ref[0] + nm_ref[...]
        _online_update([s], [vn_ref[0].astype(BF16)], m_ref, l_ref, acc_ref, first=True)
        carry_ref[...] = jnp.zeros_like(carry_ref)

    carry = carry_ref[...]
    s_parts, v_parts = [], []
    for w in range(pps):
        s_parts.append(_scores(q_rows, k_refs[w]) + (suf_refs[w][0] + carry) + hm_ref[...])
        v_parts.append(v_refs[w][...].astype(BF16))
        carry = carry + tot_refs[w][0]
    carry_ref[...] = carry
    _online_update(s_parts, v_parts, m_ref, l_ref, acc_ref, first=False)

    @pl.when(pg == pl.num_programs(1) - 1)
    def _():
        o_ref[0] = acc_ref[...] / l_ref[...]


def _fox_sample(page_table, q_rows, head_mask, new_mask, k_new, v_new, bias_new, k_pool, v_pool, suf, tot,
                page0, pps, name):
    n_batch, n_pages = page_table.shape
    rows = q_rows.shape[1]
    w = head_mask.shape[1]
    wn = new_mask.shape[1]
    per_b = lambda b, p, pt: (b, 0, 0)
    const = lambda b, p, pt: (0, 0)

    def page(slot, base):
        return lambda b, p, pt: (base + pt[b * n_pages + n_pages - 1 - (p * pps + slot)], 0)

    def page3(slot):
        return lambda b, p, pt: (pt[b * n_pages + n_pages - 1 - (p * pps + slot)], 0, 0)

    grid_spec = pltpu.PrefetchScalarGridSpec(
        num_scalar_prefetch=1,
        grid=(n_batch, n_pages // pps),
        in_specs=[pl.BlockSpec((1, rows, HEAD_DIM), per_b),
                  pl.BlockSpec((rows, w), const), pl.BlockSpec((rows, wn), const),
                  pl.BlockSpec((1, wn, HEAD_DIM), per_b), pl.BlockSpec((1, wn, HEAD_DIM), per_b),
                  pl.BlockSpec((1, 1, wn), per_b)]
                 + [pl.BlockSpec((w, HEAD_DIM), page(s, page0)) for s in range(pps)] * 2
                 + [pl.BlockSpec((1, 1, w), page3(s)) for s in range(pps)] * 2,
        out_specs=pl.BlockSpec((1, rows, HEAD_DIM), per_b),
        scratch_shapes=[pltpu.VMEM((rows, 1), F32), pltpu.VMEM((rows, 1), F32),
                        pltpu.VMEM((rows, HEAD_DIM), F32), pltpu.VMEM((1, w), F32)],
    )
    return pl.pallas_call(
        functools.partial(_fox_sample_kernel, pps=pps),
        grid_spec=grid_spec,
        out_shape=jax.ShapeDtypeStruct((n_batch, rows, HEAD_DIM), F32),
        compiler_params=_params("parallel", "arbitrary"),
        name=name,
    )(page_table.reshape(-1), q_rows, head_mask, new_mask, k_new, v_new, bias_new,
      *([k_pool] * pps), *([v_pool] * pps), *([suf] * pps), *([tot] * pps))


def _rope_kernel(x_ref, c_ref, s_ref, o_ref):
    for hh in range(x_ref.shape[1] // HEAD_DIM):
        sl = slice(hh * HEAD_DIM, (hh + 1) * HEAD_DIM)
        x = x_ref[:, sl]
        o_ref[:, sl] = x * c_ref[...] + pltpu.roll(x, HEAD_DIM // 2, 1) * s_ref[...]


def _rope(p, cos2, sin2, width, name):
    m = p.shape[0]
    tm = _pick_tile(m, ROW_TILE_CAP, 8)
    tc = _pick_tile(width, 512, HEAD_DIM)
    return pl.pallas_call(
        _rope_kernel,
        grid=(m // tm, width // tc),
        in_specs=[pl.BlockSpec((tm, tc), lambda i, j: (i, j)),
                  pl.BlockSpec((tm, HEAD_DIM), lambda i, j: (i, 0)), pl.BlockSpec((tm, HEAD_DIM), lambda i, j: (i, 0))],
        out_specs=pl.BlockSpec((tm, tc), lambda i, j: (i, j)),
        out_shape=jax.ShapeDtypeStruct((m, width), F32),
        compiler_params=_params("parallel", "arbitrary"),
        name=name,
    )(p, cos2, sin2)


def _dot_f32_nt(a, b):
    a3, b3 = _split3(a), _split3(b)
    acc = None
    for ia, ap in enumerate(a3):
        for ib, bp in enumerate(b3):
            if ia + ib > 2:
                continue
            term = lax.dot_general(ap, bp, NT_DIMS, preferred_element_type=F32)
            acc = term if acc is None else acc + term
    return acc


def _topk_mask(gate, n_blocks, topk):
    lane = lax.broadcasted_iota(jnp.int32, gate.shape, 1)
    cnt = jnp.zeros_like(gate)
    for mth in range(n_blocks):
        gm = gate[:, mth:mth + 1]
        ahead = (gm > gate) | ((gm == gate) & (mth < lane))
        cnt = cnt + jnp.where(ahead, 1.0, 0.0)
    return jnp.where((cnt < topk) & (gate > -jnp.inf), 1.0, 0.0)


def _moba_prompt_kernel(q_ref, k_ref, v_ref, o_ref, km_ref, *, blk):
    qi = pl.program_id(2)
    nb = k_ref.shape[0] // blk
    group = q_ref.shape[1] // HEAD_DIM

    @pl.when(qi == 0)
    def _():
        km_ref[...] = jnp.zeros_like(km_ref)
        for n in range(nb):
            km_ref[n:n + 1, :] = jnp.mean(k_ref[n * blk:(n + 1) * blk, :], axis=0, keepdims=True)

    heads = [q_ref[:, r * HEAD_DIM:(r + 1) * HEAD_DIM] for r in range(group)]
    q_sum = heads[0]
    for r in range(1, group):
        q_sum = q_sum + heads[r]
    gate = _dot_f32_nt(q_sum, km_ref[...])
    lane = lax.broadcasted_iota(jnp.int32, gate.shape, 1)
    gate = jnp.where(lane < qi, gate, -jnp.inf)
    drop = (1.0 - _topk_mask(gate, nb - 1, MOBA_TOPK)) * NEG

    for nk in range(1, nb + 1):
        @pl.when(qi == nk - 1)
        def _(nk=nk):
            w = nk * blk
            kb = k_ref[0:w, :].astype(BF16)
            vb = v_ref[0:w, :].astype(BF16)
            bias = [jnp.broadcast_to(drop[:, n:n + 1], (blk, blk)) for n in range(nk - 1)]
            for r in range(group):
                s = lax.dot_general(heads[r].astype(BF16), kb, NT_DIMS, preferred_element_type=F32) * ATTN_SCALE
                s = _causal_tail(s, blk)
                if nk > 1:
                    s = jnp.concatenate([s[:, n * blk:(n + 1) * blk] + bias[n] for n in range(nk - 1)]
                                        + [s[:, w - blk:]], axis=1)
                o_ref[:, r * HEAD_DIM:(r + 1) * HEAD_DIM] = _softmax_pv(s, vb).astype(o_ref.dtype)


def _moba_prompt(qk, p, n_batch, t, kv_heads, col_v, name):
    blk = MOBA_BLOCK
    gw = MOBA_GROUP * HEAD_DIM
    ck = kv_heads * MOBA_GROUP
    cv = col_v // HEAD_DIM
    assert t // blk <= LANES
    return pl.pallas_call(
        functools.partial(_moba_prompt_kernel, blk=blk),
        grid=(n_batch, kv_heads, t // blk),
        in_specs=[pl.BlockSpec((blk, gw), lambda b, g, i: (b * (t // blk) + i, g)),
                  pl.BlockSpec((t, HEAD_DIM), lambda b, g, i: (b, ck + g)),
                  pl.BlockSpec((t, HEAD_DIM), lambda b, g, i: (b, cv + g))],
        out_specs=pl.BlockSpec((blk, gw), lambda b, g, i: (b * (t // blk) + i, g)),
        out_shape=jax.ShapeDtypeStruct((n_batch * t, kv_heads * gw), BF16),
        scratch_shapes=[pltpu.VMEM((LANES, HEAD_DIM), F32)],
        compiler_params=_params("parallel", "parallel", "arbitrary"),
        name=name,
    )(qk, qk, p)


def _kmean_kernel(pt_ref, *refs, n_in, pages_per_block, kv_heads):
    del pt_ref
    o_ref = refs[n_in]
    blocks = n_in // pages_per_block
    for j in range(blocks):
        tot = None
        for w in range(pages_per_block):
            x_ref = refs[j * pages_per_block + w]
            for tok in range(PAGE_SIZE):
                part = x_ref[tok * kv_heads:(tok + 1) * kv_heads, :]
                tot = part if tot is None else tot + part
        o_ref[0, pl.program_id(1) * blocks + j] = tot / (pages_per_block * PAGE_SIZE)


def _moba_kmean(page_table, k_pool, page0, kv_heads, name):
    n_batch, n_pages = page_table.shape
    ppb = MOBA_BLOCK // PAGE_SIZE
    nb = n_pages // ppb
    blocks = _pick_tile(nb, 4, 1)
    n_in = blocks * ppb
    w = PAGE_SIZE * kv_heads

    def page(slot):
        return lambda b, n, pt: (page0 + pt[b * n_pages + n * n_in + slot], 0)

    grid_spec = pltpu.PrefetchScalarGridSpec(
        num_scalar_prefetch=1,
        grid=(n_batch, nb // blocks),
        in_specs=[pl.BlockSpec((w, HEAD_DIM), page(s)) for s in range(n_in)],
        out_specs=pl.BlockSpec((1, nb, kv_heads, HEAD_DIM), lambda b, n, pt: (b, 0, 0, 0)),
    )
    return pl.pallas_call(
        functools.partial(_kmean_kernel, n_in=n_in, pages_per_block=ppb, kv_heads=kv_heads),
        grid_spec=grid_spec,
        out_shape=jax.ShapeDtypeStruct((n_batch, nb, kv_heads, HEAD_DIM), F32),
        compiler_params=_params("parallel", "arbitrary"),
        name=name,
    )(page_table.reshape(-1), *([k_pool] * n_in))


def _moba_sample_kernel(pt_ref, q_ref, qr_ref, hm_ref, nm_ref, km_ref, kn_ref, vn_ref, *refs,
                        pps, kv_heads, n_new):
    del pt_ref
    k_refs, v_refs = refs[:pps], refs[pps:2 * pps]
    o_ref, m_ref, l_ref, acc_ref, sel_ref = refs[2 * pps:]
    pg = pl.program_id(1)
    q_rows = qr_ref[0]
    rows_per_head = MOBA_GROUP * n_new
    nb = km_ref.shape[2]
    ppb = MOBA_BLOCK // PAGE_SIZE

    @pl.when(pg == 0)
    def _():
        q = q_ref[0]
        for g in range(kv_heads):
            q_sum = q[:, g * MOBA_GROUP * HEAD_DIM:(g * MOBA_GROUP + 1) * HEAD_DIM]
            for r in range(1, MOBA_GROUP):
                c0 = (g * MOBA_GROUP + r) * HEAD_DIM
                q_sum = q_sum + q[:, c0:c0 + HEAD_DIM]
            sel = _topk_mask(_dot_f32_nt(q_sum, km_ref[0, g]), nb, MOBA_TOPK)
            for r in range(MOBA_GROUP):
                r0 = g * rows_per_head + r * n_new
                sel_ref[r0:r0 + n_new, 0:nb] = sel
        s = _scores(q_rows, kn_ref.at[0]) + nm_ref[...]
        _online_update([s], [vn_ref[0].astype(BF16)], m_ref, l_ref, acc_ref, first=True)

    lane = lax.broadcasted_iota(jnp.int32, (q_rows.shape[0], nb), 1)
    s_parts, v_parts = [], []
    for w in range(pps):
        if w % ppb == 0:
            blk = (pg * pps + w) // ppb
            keep = jnp.sum(jnp.where(lane == blk, sel_ref[:, 0:nb], 0.0), axis=1, keepdims=True)
            drop = (1.0 - keep) * NEG
        s_parts.append(_scores(q_rows, k_refs[w]) + drop + hm_ref[...])
        v_parts.append(v_refs[w][...].astype(BF16))
    _online_update(s_parts, v_parts, m_ref, l_ref, acc_ref, first=False)

    @pl.when(pg == pl.num_programs(1) - 1)
    def _():
        o_ref[0] = acc_ref[...] / l_ref[...]


def _moba_sample(page_table, q, q_rows, head_mask, new_mask, k_mean, k_new, v_new, k_pool, v_pool, page0, pps,
                 kv_heads, n_new, name):
    n_batch, n_pages = page_table.shape
    rows = q_rows.shape[1]
    w = head_mask.shape[1]
    wn = new_mask.shape[1]
    nb = k_mean.shape[2]
    assert nb <= LANES and pps % (MOBA_BLOCK // PAGE_SIZE) == 0
    per_b = lambda b, p, pt: (b, 0, 0)
    const = lambda b, p, pt: (0, 0)

    def page(slot):
        return lambda b, p, pt: (page0 + pt[b * n_pages + p * pps + slot], 0)

    grid_spec = pltpu.PrefetchScalarGridSpec(
        num_scalar_prefetch=1,
        grid=(n_batch, n_pages // pps),
        in_specs=[pl.BlockSpec((1, n_new, q.shape[2]), per_b), pl.BlockSpec((1, rows, HEAD_DIM), per_b),
                  pl.BlockSpec((rows, w), const), pl.BlockSpec((rows, wn), const),
                  pl.BlockSpec((1, kv_heads, nb, HEAD_DIM), lambda b, p, pt: (b, 0, 0, 0)),
                  pl.BlockSpec((1, wn, HEAD_DIM), per_b), pl.BlockSpec((1, wn, HEAD_DIM), per_b)]
                 + [pl.BlockSpec((w, HEAD_DIM), page(s)) for s in range(pps)] * 2,
        out_specs=pl.BlockSpec((1, rows, HEAD_DIM), per_b),
        scratch_shapes=[pltpu.VMEM((rows, 1), F32), pltpu.VMEM((rows, 1), F32),
                        pltpu.VMEM((rows, HEAD_DIM), F32), pltpu.VMEM((rows, LANES), F32)],
    )
    return pl.pallas_call(
        functools.partial(_moba_sample_kernel, pps=pps, kv_heads=kv_heads, n_new=n_new),
        grid_spec=grid_spec,
        out_shape=jax.ShapeDtypeStruct((n_batch, rows, HEAD_DIM), F32),
        compiler_params=_params("parallel", "arbitrary"),
        name=name,
    )(page_table.reshape(-1), q, q_rows, head_mask, new_mask, k_mean, k_new, v_new,
      *([k_pool] * pps), *([v_pool] * pps))


def _cross_kernel(q_ref, k_ref, v_ref, o_ref):
    s = lax.dot_general(q_ref[...].astype(BF16), k_ref[...].astype(BF16), NT_DIMS, preferred_element_type=F32) * CROSS_SCALE
    m = jnp.max(s, axis=1, keepdims=True)
    p = jnp.exp(s - m)
    l = jnp.sum(p, axis=1, keepdims=True)
    o = jnp.dot(p.astype(BF16), v_ref[...].astype(BF16), preferred_element_type=F32)
    o_ref[...] = (o / l).astype(o_ref.dtype)


def _cross(q, mem_k, mem_v, row0, n_batch, t, out_dtype, name):
    mem = mem_k.shape[0] // n_batch
    tq = _pick_tile(t, 512, 8)
    nq = t // tq
    r0 = row0 // tq
    return pl.pallas_call(
        _cross_kernel,
        grid=(n_batch, nq, CROSS_HEADS),
        in_specs=[pl.BlockSpec((tq, CROSS_HEAD_DIM), lambda b, i, h: (r0 + b * nq + i, h)),
                  pl.BlockSpec((mem, CROSS_HEAD_DIM), lambda b, i, h: (b, h)),
                  pl.BlockSpec((mem, CROSS_HEAD_DIM), lambda b, i, h: (b, h))],
        out_specs=pl.BlockSpec((tq, CROSS_HEAD_DIM), lambda b, i, h: (b * nq + i, h)),
        out_shape=jax.ShapeDtypeStruct((n_batch * t, CROSS_HEADS * CROSS_HEAD_DIM), out_dtype),
        compiler_params=_params("parallel", "parallel", "parallel"),
        name=name,
    )(q, mem_k, mem_v)


def _head_rows(q):
    nb, t, w = q.shape
    return q.reshape(nb, t, w // HEAD_DIM, HEAD_DIM).transpose(0, 2, 1, 3).reshape(nb, -1, HEAD_DIM).astype(BF16)


def _decode_masks(q_heads, kv_heads, n_new):
    row = jnp.arange(q_heads * n_new)
    own = ((row // n_new) // (q_heads // kv_heads))[:, None]
    t_q = (row % n_new)[:, None]
    lane = jnp.arange(PAGE_SIZE * kv_heads)[None, :]
    head_mask = jnp.where(lane % kv_heads == own, 0.0, NEG).astype(F32)
    lane = jnp.arange(n_new * kv_heads)[None, :]
    new_mask = jnp.where((lane % kv_heads == own) & (lane // kv_heads <= t_q), 0.0, NEG).astype(F32)
    return head_mask, new_mask


def kernel(x_prompt, x_sample, cache_fox_k, cache_fox_v, cache_fox_logf, state_conv, cache_moba_k, cache_moba_v, cache_mem_k, cache_mem_v, page_table, mem_prompt, w_in_even, b_forget, w_conv, w_out_even, w_in_odd, w_out_odd, w_cross_q, w_cross_k, w_cross_v, w_cross_o, w_ffa_gate, w_ffa_up, w_ffa_down, w_ffb_gate, w_ffb_up, w_ffb_down, ln_g, ln_b):
    bp, t, d = x_prompt.shape
    bs, ts, _ = x_sample.shape
    depth = ln_g.shape[0]
    mp, ms = bp * t, bs * ts
    m = mp + ms
    alpha = (2 * depth) ** 0.25
    conv_ch = d // 2
    fox_heads = conv_ch // HEAD_DIM
    fox_w = fox_heads * HEAD_DIM
    moba_heads = d // HEAD_DIM
    kv_heads = moba_heads // MOBA_GROUP
    kv_w = kv_heads * HEAD_DIM
    cross_w = CROSS_HEADS * CROSS_HEAD_DIM
    mem_len = mem_prompt.shape[1]
    n_pages = page_table.shape[1]
    past_len = n_pages * PAGE_SIZE
    new_pad = BF16_SUBLANES

    def ffn_weights(wg, wu, wd):
        return wg.astype(BF16), wu.astype(BF16), wd.astype(BF16)

    def ffn(act, wts, g, b, name):
        wg, wu, wd = wts
        hidden = _proj(act, (wg, wu), BF16, name + "_swiglu")
        return _resid(hidden, wd, act, g, b, alpha, 0.5, name + "_down", tn_cap=256)

    pos = jnp.concatenate([jnp.tile(jnp.arange(t, dtype=jnp.int32), bp),
                           jnp.tile(past_len + jnp.arange(ts, dtype=jnp.int32), bs)])
    half = HEAD_DIM // 2
    inv_freq = ROPE_THETA ** (-jnp.arange(half, dtype=F32) / half)
    ang = pos.astype(F32)[:, None] * inv_freq[None, :]
    cos2 = jnp.concatenate([jnp.cos(ang), jnp.cos(ang)], axis=1)
    sin2 = jnp.concatenate([-jnp.sin(ang), jnp.sin(ang)], axis=1)

    ones = jnp.ones((1, d), F32)
    zeros = jnp.zeros((1, d), F32)
    unit_stats = lambda rows: jnp.zeros((rows, LANES), F32).at[:, 1].set(1.0)

    mem_act = (mem_prompt.reshape(bp * mem_len, d), unit_stats(bp * mem_len), ones, zeros)
    mem_k_p = jnp.stack([_proj(mem_act, (w_cross_k[i].astype(BF16),), F32, f"mem_k{i}") for i in range(depth)])
    mem_v_p = jnp.stack([_proj(mem_act, (w_cross_v[i].astype(BF16),), F32, f"mem_v{i}") for i in range(depth)])

    x0 = jnp.concatenate([x_prompt.reshape(mp, d), x_sample.reshape(ms, d)], axis=0)
    act = (x0, unit_stats(m), ones, zeros)
    st_p = {name: [] for name in ("fox_k", "fox_v", "fox_logf", "conv", "moba_k", "moba_v")}
    st_s = {name: [] for name in st_p}

    for i in range(depth):
        g = [ln_g[i, s].reshape(1, d) for s in range(4)]
        b = [ln_b[i, s].reshape(1, d) for s in range(4)]
        j = i // 2
        act = ffn(act, ffn_weights(w_ffa_gate[i], w_ffa_up[i], w_ffa_down[i]), g[0], b[0], f"ffa{i}")

        if i % 2 == 0:
            n_main = 3 * conv_ch + 3 * fox_w
            w_in = w_in_even[j]
            p = _proj(act, (w_in[:, :n_main].astype(BF16),), F32, f"in_even{i}")
            w_f = jnp.pad(w_in[:, n_main:].astype(BF16), ((0, 0), (0, LANES - fox_heads)))
            f = _proj(act, (w_f,), F32, f"in_forget{i}")
            b_f = jnp.pad(b_forget[j].astype(F32), (0, LANES - fox_heads)).reshape(1, LANES)
            col_q, col_k, col_v = 3 * conv_ch, 3 * conv_ch + fox_w, 3 * conv_ch + 2 * fox_w

            a_p, conv_p = _conv(p, jnp.zeros((bp, CONV_WIDTH - 1, conv_ch), F32), w_conv[j], 0, bp, t, conv_ch,
                                BF16, f"conv_p{i}")
            a_s, conv_s = _conv(p, state_conv[j], w_conv[j], mp, bs, ts, conv_ch, F32, f"conv_s{i}")
            a_s = a_s.astype(BF16)

            lf_p, nc_p = _logf(f, b_f, 0, bp, t, f"logf_p{i}")
            lf_s, nc_s = _logf(f, b_f, mp, bs, ts, f"logf_s{i}")
            tq = _pick_tile(t, 256, LANES)
            nct = nc_p[:, :fox_heads].reshape(bp, t, fox_heads).transpose(0, 2, 1).reshape(bp, fox_heads, 1, t)
            o_p = _fox_prompt(p, nct, bp, t, fox_heads, col_q, tq, f"fox_p{i}")

            ps = p[mp:]
            q_s = ps[:, col_q:col_q + fox_w].reshape(bs, ts, fox_w)
            k_s = ps[:, col_k:col_k + fox_w].reshape(bs, ts, fox_w)
            v_s = ps[:, col_v:col_v + fox_w].reshape(bs, ts, fox_w)
            head_mask, new_mask = _decode_masks(fox_heads, fox_heads, ts)
            bias_new = nc_s[:, :fox_heads].reshape(bs, 1, ts * fox_heads)
            n_phys = cache_fox_k.shape[1]
            lf_suffix, lf_total = _lf_scan(cache_fox_logf[j].reshape(n_phys, PAGE_SIZE * fox_heads), fox_heads,
                                           f"lf_scan{i}")
            o_s = _fox_sample(page_table, _head_rows(q_s), head_mask, new_mask,
                              k_s.reshape(bs, ts * fox_heads, HEAD_DIM), v_s.reshape(bs, ts * fox_heads, HEAD_DIM),
                              bias_new, cache_fox_k.reshape(-1, HEAD_DIM), cache_fox_v.reshape(-1, HEAD_DIM),
                              lf_suffix.reshape(n_phys, 1, -1), lf_total.reshape(n_phys, 1, -1),
                              j * n_phys, FOX_PAGES_PER_STEP, f"fox_s{i}")
            o_s = o_s.reshape(bs, fox_heads, ts, HEAD_DIM).transpose(0, 2, 1, 3).reshape(ms, fox_w).astype(BF16)

            mix_in = jnp.concatenate([jnp.concatenate([a_p, o_p], axis=1), jnp.concatenate([a_s, o_s], axis=1)], axis=0)
            w_out = w_out_even[j].astype(BF16)

            st_p["fox_k"].append(p[:mp, col_k:col_k + fox_w].reshape(bp, t, fox_heads, HEAD_DIM))
            st_p["fox_v"].append(p[:mp, col_v:col_v + fox_w].reshape(bp, t, fox_heads, HEAD_DIM))
            st_p["fox_logf"].append(lf_p[:, :fox_heads].reshape(bp, t, fox_heads))
            st_p["conv"].append(conv_p)
            st_s["fox_k"].append(k_s.reshape(bs, ts, fox_heads, HEAD_DIM))
            st_s["fox_v"].append(v_s.reshape(bs, ts, fox_heads, HEAD_DIM))
            st_s["fox_logf"].append(lf_s[:, :fox_heads].reshape(bs, ts, fox_heads))
            st_s["conv"].append(conv_s)
        else:
            p = _proj(act, (w_in_odd[j].astype(BF16),), F32, f"in_odd{i}")
            qk_w = d + kv_w
            qk = _rope(p, cos2, sin2, qk_w, f"rope{i}")
            o_p = _moba_prompt(qk, p, bp, t, kv_heads, qk_w, f"moba_p{i}")

            q_s = qk[mp:, :d].reshape(bs, ts, d)
            k_s = qk[mp:, d:].reshape(bs, ts, kv_w)
            v_s = p[mp:, qk_w:].reshape(bs, ts, kv_w)
            head_mask, new_mask = _decode_masks(moba_heads, kv_heads, ts)
            n_phys = cache_moba_k.shape[1]
            k_pool = cache_moba_k.reshape(-1, HEAD_DIM)
            v_pool = cache_moba_v.reshape(-1, HEAD_DIM)
            k_mean = _moba_kmean(page_table, k_pool, j * n_phys, kv_heads, f"moba_kmean{i}").transpose(0, 2, 1, 3)
            o_s = _moba_sample(page_table, q_s, _head_rows(q_s), head_mask, new_mask, k_mean,
                               k_s.reshape(bs, ts * kv_heads, HEAD_DIM), v_s.reshape(bs, ts * kv_heads, HEAD_DIM),
                               k_pool, v_pool, j * n_phys, MOBA_PAGES_PER_STEP, kv_heads, ts, f"moba_s{i}")
            o_s = o_s.reshape(bs, moba_heads, ts, HEAD_DIM).transpose(0, 2, 1, 3).reshape(ms, d).astype(BF16)

            mix_in = jnp.concatenate([o_p, o_s], axis=0)
            w_out = w_out_odd[j].astype(BF16)

            st_p["moba_k"].append(qk[:mp, d:].reshape(bp, t, kv_heads, HEAD_DIM))
            st_p["moba_v"].append(p[:mp, qk_w:].reshape(bp, t, kv_heads, HEAD_DIM))
            st_s["moba_k"].append(k_s.reshape(bs, ts, kv_heads, HEAD_DIM))
            st_s["moba_v"].append(v_s.reshape(bs, ts, kv_heads, HEAD_DIM))

        act = _resid(mix_in, w_out, act, g[1], b[1], alpha, 1.0, f"mix_out{i}")

        q = _proj(act, (w_cross_q[i].astype(BF16),), F32, f"cross_q{i}")
        o_p = _cross(q, mem_k_p[i], mem_v_p[i], 0, bp, t, BF16, f"cross_p{i}")
        o_s = _cross(q, cache_mem_k[i].reshape(bs * mem_len, cross_w), cache_mem_v[i].reshape(bs * mem_len, cross_w),
                     mp, bs, ts, F32, f"cross_s{i}").astype(BF16)
        act = _resid(jnp.concatenate([o_p, o_s], axis=0), w_cross_o[i].astype(BF16), act, g[2], b[2], alpha, 1.0,
                     f"cross_o{i}")

        act = ffn(act, ffn_weights(w_ffb_gate[i], w_ffb_up[i], w_ffb_down[i]), g[3], b[3], f"ffb{i}")

    y_prompt = _final_norm(act, 0, mp, "y_prompt").reshape(bp, t, d)
    y_sample = _final_norm(act, mp, ms, "y_sample").reshape(bs, ts, d)
    stack = lambda xs: jnp.stack(xs)
    mem_shape = (depth, bp, mem_len, CROSS_HEADS, CROSS_HEAD_DIM)
    return (y_prompt, y_sample,
            stack(st_p["fox_k"]), stack(st_p["fox_v"]), stack(st_p["fox_logf"]), stack(st_p["conv"]),
            stack(st_p["moba_k"]), stack(st_p["moba_v"]),
            mem_k_p.reshape(mem_shape), mem_v_p.reshape(mem_shape),
            stack(st_s["fox_k"]), stack(st_s["fox_v"]), stack(st_s["fox_logf"]), stack(st_s["conv"]),
            stack(st_s["moba_k"]), stack(st_s["moba_v"]))
```

```python
import functools
import math

import jax
import jax.numpy as jnp
from jax import lax
from jax.experimental import pallas as pl
from jax.experimental.pallas import tpu as pltpu

F32 = jnp.float32
BF16 = jnp.bfloat16

HEAD_DIM = 128
CONV_WIDTH = 3
PAGE_SIZE = 128
MOBA_BLOCK = 256
MOBA_TOPK = 3
MOBA_GROUP = 4
CROSS_HEADS = 4
CROSS_HEAD_DIM = 256
ROPE_THETA = 10000.0
LN_EPS = 1e-5
ATTN_SCALE = HEAD_DIM ** -0.5
CROSS_SCALE = CROSS_HEAD_DIM ** -0.5

LANES = 128
BF16_SUBLANES = 16
VMEM_LIMIT_BYTES = 56 * 1024 * 1024
ROW_TILE_CAP = 704
COL_TILE_CAP = 512
CAST_BLOCK_BYTES = 8 * 1024 * 1024
FOX_PAGES_PER_STEP = 4
MOBA_PAGES_PER_STEP = 8
NEG = -1e30
NT_DIMS = (((1,), (1,)), ((), ()))


def _pick_tile(n, cap, mult):
    best = None
    for d in range(mult, min(n, cap) + 1, mult):
        if n % d == 0:
            best = d
    if best is None:
        raise ValueError(f"no tile for {n} (cap {cap}, multiple of {mult})")
    return best


def _params(*sem):
    return pltpu.CompilerParams(dimension_semantics=sem, vmem_limit_bytes=VMEM_LIMIT_BYTES)


def _normalize(z, st, g, b):
    return (z - st[:, 0:1]) * st[:, 1:2] * g + b


def _split3(x):
    hi = x.astype(BF16)
    r = x - hi.astype(F32)
    mid = r.astype(BF16)
    lo = (r - mid.astype(F32)).astype(BF16)
    return hi, mid, lo


def _dot01(a01, x, dims=(((1,), (0,)), ((), ()))):
    hi, mid, lo = _split3(x)
    acc = lax.dot_general(a01, hi, dims, preferred_element_type=F32)
    acc = acc + lax.dot_general(a01, mid, dims, preferred_element_type=F32)
    return acc + lax.dot_general(a01, lo, dims, preferred_element_type=F32)


def _proj_kernel(z_ref, st_ref, g_ref, b_ref, *rest, n_w):
    w_refs, o_ref, xb_ref = rest[:n_w], rest[n_w], rest[n_w + 1]

    @pl.when(pl.program_id(1) == 0)
    def _():
        xb_ref[...] = _normalize(z_ref[...], st_ref[...], g_ref[...], b_ref[...]).astype(BF16)

    xb = xb_ref[...]
    a = jnp.dot(xb, w_refs[0][...], preferred_element_type=F32)
    if n_w == 2:
        c = jnp.dot(xb, w_refs[1][...], preferred_element_type=F32)
        a = a * (1.0 / (1.0 + jnp.exp(-a))) * c
    o_ref[...] = a.astype(o_ref.dtype)


def _proj(act, ws, layer, n, out_dtype, name, tn_cap=COL_TILE_CAP):
    z, st, g, b = act
    m, k = z.shape
    tm = _pick_tile(m, ROW_TILE_CAP, BF16_SUBLANES)
    tn = _pick_tile(n, tn_cap, LANES)
    row = lambda i, j: (i, 0)
    return pl.pallas_call(
        functools.partial(_proj_kernel, n_w=len(ws)),
        grid=(m // tm, n // tn),
        in_specs=[pl.BlockSpec((tm, k), row), pl.BlockSpec((tm, LANES), row),
                  pl.BlockSpec((1, k), lambda i, j: (0, 0)), pl.BlockSpec((1, k), lambda i, j: (0, 0))]
                 + [pl.BlockSpec((None, k, tn), lambda i, j: (layer, 0, j)) for _ in ws],
        out_specs=pl.BlockSpec((tm, tn), lambda i, j: (i, j)),
        out_shape=jax.ShapeDtypeStruct((m, n), out_dtype),
        scratch_shapes=[pltpu.VMEM((tm, k), BF16)],
        compiler_params=_params("parallel", "arbitrary"),
        name=name,
    )(z, st, g, b, *ws)


def _resid_kernel(a_ref, w_ref, zr_ref, sr_ref, gr_ref, br_ref, zo_ref, so_ref, mean_ref, m2_ref,
                  *, alpha, scale, n_total):
    j = pl.program_id(1)
    nj = n_total // zo_ref.shape[1]
    y = jnp.dot(a_ref[...], w_ref[...], preferred_element_type=F32)
    x_res = _normalize(zr_ref[...], sr_ref[...], gr_ref[...], br_ref[...])
    z_new = alpha * x_res + scale * y
    zo_ref[...] = z_new
    tn = z_new.shape[1]
    mu_t = jnp.mean(z_new, axis=1, keepdims=True)
    d = z_new - mu_t
    m2_t = jnp.sum(d * d, axis=1, keepdims=True)
    lane = lax.broadcasted_iota(jnp.int32, mean_ref.shape, 1)

    @pl.when(j == 0)
    def _():
        mean_ref[...] = jnp.zeros_like(mean_ref)
        m2_ref[...] = jnp.zeros_like(m2_ref)

    mean_ref[...] = jnp.where(lane == j, mu_t, mean_ref[...])
    m2_ref[...] = jnp.where(lane == j, m2_t, m2_ref[...])

    @pl.when(j == nj - 1)
    def _():
        mus = mean_ref[...]
        mean = jnp.sum(mus, axis=1, keepdims=True) / nj
        dev = jnp.where(lane < nj, mus - mean, 0.0)
        m2 = jnp.sum(m2_ref[...], axis=1, keepdims=True) + tn * jnp.sum(dev * dev, axis=1, keepdims=True)
        rstd = lax.rsqrt(m2 / n_total + LN_EPS)
        so_ref[...] = jnp.where(lane == 0, mean, jnp.where(lane == 1, rstd, 0.0))


def _resid(a, w, layer, act, ln_g, ln_b, alpha, scale, name, tn_cap=COL_TILE_CAP):
    z, st, g, b = act
    m, k = a.shape
    n = w.shape[2]
    tm = _pick_tile(m, ROW_TILE_CAP, BF16_SUBLANES)
    tn = _pick_tile(n, tn_cap, LANES)
    assert n // tn <= LANES
    z_new, st_new = pl.pallas_call(
        functools.partial(_resid_kernel, alpha=alpha, scale=scale, n_total=n),
        grid=(m // tm, n // tn),
        in_specs=[pl.BlockSpec((tm, k), lambda i, j: (i, 0)), pl.BlockSpec((None, k, tn), lambda i, j: (layer, 0, j)),
                  pl.BlockSpec((tm, tn), lambda i, j: (i, j)), pl.BlockSpec((tm, LANES), lambda i, j: (i, 0)),
                  pl.BlockSpec((1, tn), lambda i, j: (0, j)), pl.BlockSpec((1, tn), lambda i, j: (0, j))],
        out_specs=[pl.BlockSpec((tm, tn), lambda i, j: (i, j)), pl.BlockSpec((tm, LANES), lambda i, j: (i, 0))],
        out_shape=[jax.ShapeDtypeStruct((m, n), F32), jax.ShapeDtypeStruct((m, LANES), F32)],
        scratch_shapes=[pltpu.VMEM((tm, LANES), F32), pltpu.VMEM((tm, LANES), F32)],
        compiler_params=_params("parallel", "arbitrary"),
        name=name,
    )(a, w, z, st, g, b)
    return z_new, st_new, ln_g, ln_b


def _cast_kernel(x_ref, o_ref):
    o_ref[...] = x_ref[...].astype(o_ref.dtype)


def _to_bf16(w, name):
    l, k, n = w.shape
    rows = l * k
    tr = _pick_tile(rows, max(BF16_SUBLANES, CAST_BLOCK_BYTES // (4 * n)), BF16_SUBLANES)
    spec = pl.BlockSpec((tr, n), lambda i: (i, 0))
    out = pl.pallas_call(
        _cast_kernel,
        grid=(rows // tr,),
        in_specs=[spec],
        out_specs=spec,
        out_shape=jax.ShapeDtypeStruct((rows, n), BF16),
        compiler_params=_params("parallel"),
        name=name,
    )(w.reshape(rows, n))
    return out.reshape(l, k, n)


def _final_norm_kernel(z_ref, st_ref, g_ref, b_ref, o_ref):
    o_ref[...] = _normalize(z_ref[...], st_ref[...], g_ref[...], b_ref[...])


def _final_norm(act, row0, rows, name):
    z, st, g, b = act
    d = z.shape[1]
    tm = _pick_tile(math.gcd(rows, row0) if row0 else rows, 512, 8)
    off = row0 // tm
    return pl.pallas_call(
        _final_norm_kernel,
        grid=(rows // tm,),
        in_specs=[pl.BlockSpec((tm, d), lambda i: (i + off, 0)), pl.BlockSpec((tm, LANES), lambda i: (i + off, 0)),
                  pl.BlockSpec((1, d), lambda i: (0, 0)), pl.BlockSpec((1, d), lambda i: (0, 0))],
        out_specs=pl.BlockSpec((tm, d), lambda i: (i, 0)),
        out_shape=jax.ShapeDtypeStruct((rows, d), F32),
        compiler_params=_params("parallel"),
        name=name,
    )(z, st, g, b)


def _conv_kernel(ain_ref, ab_ref, ac_ref, buf_ref, w_ref, o_ref, nb_ref):
    u = ac_ref[...] * ain_ref[...]
    t = u.shape[0]
    row = lax.broadcasted_iota(jnp.int32, u.shape, 0)
    b0 = buf_ref[0, 0:1, :]
    b1 = buf_ref[0, 1:2, :]
    u1 = jnp.where(row == 0, b1, pltpu.roll(u, 1, 0))
    u2 = jnp.where(row == 0, b0, jnp.where(row == 1, b1, pltpu.roll(u, 2, 0)))
    y = w_ref[0:1, :] * u2 + w_ref[1:2, :] * u1 + w_ref[2:3, :] * u
    o_ref[...] = (ab_ref[...] * y).astype(o_ref.dtype)
    nb_ref[0] = u[t - (CONV_WIDTH - 1):, :]


def _conv(p, buf, w, row0, n_batch, t, ch, out_dtype, name):
    tc = _pick_tile(ch, 512, LANES)
    nc = ch // tc
    r0 = row0 // t
    return pl.pallas_call(
        _conv_kernel,
        grid=(n_batch, nc),
        in_specs=[pl.BlockSpec((t, tc), lambda b, c: (b + r0, c)),
                  pl.BlockSpec((t, tc), lambda b, c: (b + r0, c + nc)),
                  pl.BlockSpec((t, tc), lambda b, c: (b + r0, c + 2 * nc)),
                  pl.BlockSpec((1, CONV_WIDTH - 1, tc), lambda b, c: (b, 0, c)),
                  pl.BlockSpec((CONV_WIDTH, tc), lambda b, c: (0, c))],
        out_specs=[pl.BlockSpec((t, tc), lambda b, c: (b, c)),
                   pl.BlockSpec((1, CONV_WIDTH - 1, tc), lambda b, c: (b, 0, c))],
        out_shape=[jax.ShapeDtypeStruct((n_batch * t, ch), out_dtype),
                   jax.ShapeDtypeStruct((n_batch, CONV_WIDTH - 1, ch), F32)],
        compiler_params=_params("parallel", "parallel"),
        name=name,
    )(p, p, p, buf, w)


def _logf_kernel(f_ref, bf_ref, lf_ref, nc_ref):
    x = f_ref[...] + bf_ref[...]
    lf = jnp.minimum(x, 0.0) - jnp.log1p(jnp.exp(-jnp.abs(x)))
    lf_ref[...] = lf
    row = lax.broadcasted_iota(jnp.int32, lf.shape, 0)
    cs = lf
    shift = 1
    while shift < lf.shape[0]:
        cs = cs + jnp.where(row >= shift, pltpu.roll(cs, shift, 0), 0.0)
        shift *= 2
    nc_ref[...] = -cs


def _logf(f, bias, row0, n_batch, t, name):
    r0 = row0 // t
    return pl.pallas_call(
        _logf_kernel,
        grid=(n_batch,),
        in_specs=[pl.BlockSpec((t, LANES), lambda b: (b + r0, 0)), pl.BlockSpec((1, LANES), lambda b: (0, 0))],
        out_specs=[pl.BlockSpec((t, LANES), lambda b: (b, 0)), pl.BlockSpec((t, LANES), lambda b: (b, 0))],
        out_shape=[jax.ShapeDtypeStruct((n_batch * t, LANES), F32)] * 2,
        compiler_params=_params("parallel"),
        name=name,
    )(f, bias)


def _softmax_pv(s, vb):
    m = jnp.max(s, axis=1, keepdims=True)
    p = jnp.exp(s - m)
    l = jnp.sum(p, axis=1, keepdims=True)
    return jnp.dot(p.astype(BF16), vb, preferred_element_type=F32) / l


def _causal_tail(s, width):
    row = lax.broadcasted_iota(jnp.int32, (s.shape[0], width), 0)
    col = lax.broadcasted_iota(jnp.int32, (s.shape[0], width), 1)
    tail = jnp.where(col <= row, s[:, s.shape[1] - width:], NEG)
    if s.shape[1] == width:
        return tail
    return jnp.concatenate([s[:, :s.shape[1] - width], tail], axis=1)


def _fox_prompt_kernel(q_ref, k_ref, v_ref, nc_ref, o_ref, *, tq):
    h = pl.program_id(1)
    qi = pl.program_id(2)
    qb = q_ref[...].astype(BF16)
    for nk in range(1, k_ref.shape[0] // tq + 1):
        @pl.when(qi == nk - 1)
        def _(nk=nk):
            w = nk * tq
            kb = k_ref[0:w, :].astype(BF16)
            vb = v_ref[0:w, :].astype(BF16)
            s = lax.dot_general(qb, kb, NT_DIMS, preferred_element_type=F32) * ATTN_SCALE + nc_ref[0, h, :, 0:w]
            o_ref[...] = _softmax_pv(_causal_tail(s, tq), vb).astype(o_ref.dtype)


def _fox_prompt(p, nct, n_batch, t, heads, col_q, tq, name):
    cq = col_q // HEAD_DIM
    return pl.pallas_call(
        functools.partial(_fox_prompt_kernel, tq=tq),
        grid=(n_batch, heads, t // tq),
        in_specs=[pl.BlockSpec((tq, HEAD_DIM), lambda b, h, i: (b * (t // tq) + i, cq + h)),
                  pl.BlockSpec((t, HEAD_DIM), lambda b, h, i: (b, cq + heads + h)),
                  pl.BlockSpec((t, HEAD_DIM), lambda b, h, i: (b, cq + 2 * heads + h)),
                  pl.BlockSpec((1, heads, 1, t), lambda b, h, i: (b, 0, 0, 0))],
        out_specs=pl.BlockSpec((tq, HEAD_DIM), lambda b, h, i: (b * (t // tq) + i, h)),
        out_shape=jax.ShapeDtypeStruct((n_batch * t, heads * HEAD_DIM), BF16),
        compiler_params=_params("parallel", "parallel", "arbitrary"),
        name=name,
    )(p, p, p, nct)


def _online_update(s_parts, v_parts, m_ref, l_ref, acc_ref, first):
    m_new = jnp.max(s_parts[0], axis=1, keepdims=True)
    for s in s_parts[1:]:
        m_new = jnp.maximum(m_new, jnp.max(s, axis=1, keepdims=True))
    if not first:
        m_new = jnp.maximum(m_ref[...], m_new)
    l_new, pv = None, None
    for s, v in zip(s_parts, v_parts):
        p = jnp.exp(s - m_new)
        l_part = jnp.sum(p, axis=1, keepdims=True)
        pv_part = jnp.dot(p.astype(BF16), v, preferred_element_type=F32)
        l_new = l_part if l_new is None else l_new + l_part
        pv = pv_part if pv is None else pv + pv_part
    if first:
        l_ref[...] = l_new
        acc_ref[...] = pv
    else:
        corr = jnp.exp(m_ref[...] - m_new)
        l_ref[...] = l_ref[...] * corr + l_new
        acc_ref[...] = acc_ref[...] * corr + pv
    m_ref[...] = m_new


def _scores(q_rows, k_ref):
    return lax.dot_general(q_rows, k_ref[...].astype(BF16), NT_DIMS, preferred_element_type=F32) * ATTN_SCALE


def _lf_scan_kernel(lf_ref, suf_ref, tot_ref, *, heads):
    lf = lf_ref[...]
    w = lf.shape[1]
    lane = lax.broadcasted_iota(jnp.int32, lf.shape, 1)
    a = lf
    shift = heads
    while shift < w:
        a = a + jnp.where(lane + shift < w, pltpu.roll(a, w - shift, 1), 0.0)
        shift *= 2
    suf_ref[...] = jnp.where(lane + heads < w, pltpu.roll(a, w - heads, 1), 0.0)
    tot = jnp.where(lane < heads, a, 0.0)
    shift = heads
    while shift < w:
        tot = tot + pltpu.roll(tot, shift, 1)
        shift *= 2
    tot_ref[...] = tot


def _lf_scan(lf2, heads, name):
    n_phys, w = lf2.shape
    tp = _pick_tile(n_phys, 256, 8)
    spec = pl.BlockSpec((tp, w), lambda i: (i, 0))
    return pl.pallas_call(
        functools.partial(_lf_scan_kernel, heads=heads),
        grid=(n_phys // tp,),
        in_specs=[spec],
        out_specs=[spec, spec],
        out_shape=[jax.ShapeDtypeStruct((n_phys, w), F32)] * 2,
        compiler_params=_params("parallel"),
        name=name,
    )(lf2)


def _fox_sample_kernel(pt_ref, q_ref, hm_ref, nm_ref, kn_ref, vn_ref, bn_ref, *refs, pps):
    del pt_ref
    k_refs, v_refs = refs[:pps], refs[pps:2 * pps]
    suf_refs, tot_refs = refs[2 * pps:3 * pps], refs[3 * pps:4 * pps]
    o_ref, m_ref, l_ref, acc_ref, carry_ref = refs[4 * pps:]
    pg = pl.program_id(1)
    q_rows = q_ref[0]

    @pl.when(pg == 0)
    def _():
        s = _scores(q_rows, kn_ref.at[0]) + bn_ref[0] + nm_ref[...]
        _online_update([s], [vn_ref[0].astype(BF16)], m_ref, l_ref, acc_ref, first=True)
        carry_ref[...] = jnp.zeros_like(carry_ref)

    carry = carry_ref[...]
    s_parts, v_parts = [], []
    for w in range(pps):
        s_parts.append(_scores(q_rows, k_refs[w]) + (suf_refs[w][0] + carry) + hm_ref[...])
        v_parts.append(v_refs[w][...].astype(BF16))
        carry = carry + tot_refs[w][0]
    carry_ref[...] = carry
    _online_update(s_parts, v_parts, m_ref, l_ref, acc_ref, first=False)

    @pl.when(pg == pl.num_programs(1) - 1)
    def _():
        o_ref[0] = acc_ref[...] / l_ref[...]


def _fox_sample(page_table, q_rows, head_mask, new_mask, k_new, v_new, bias_new, k_pool, v_pool, suf, tot,
                page0, pps, name):
    n_batch, n_pages = page_table.shape
    rows = q_rows.shape[1]
    w = head_mask.shape[1]
    wn = new_mask.shape[1]
    per_b = lambda b, p, pt: (b, 0, 0)
    const = lambda b, p, pt: (0, 0)

    def page(slot, base):
        return lambda b, p, pt: (base + pt[b * n_pages + n_pages - 1 - (p * pps + slot)], 0)

    def page3(slot):
        return lambda b, p, pt: (pt[b * n_pages + n_pages - 1 - (p * pps + slot)], 0, 0)

    grid_spec = pltpu.PrefetchScalarGridSpec(
        num_scalar_prefetch=1,
        grid=(n_batch, n_pages // pps),
        in_specs=[pl.BlockSpec((1, rows, HEAD_DIM), per_b),
                  pl.BlockSpec((rows, w), const), pl.BlockSpec((rows, wn), const),
                  pl.BlockSpec((1, wn, HEAD_DIM), per_b), pl.BlockSpec((1, wn, HEAD_DIM), per_b),
                  pl.BlockSpec((1, 1, wn), per_b)]
                 + [pl.BlockSpec((w, HEAD_DIM), page(s, page0)) for s in range(pps)] * 2
                 + [pl.BlockSpec((1, 1, w), page3(s)) for s in range(pps)] * 2,
        out_specs=pl.BlockSpec((1, rows, HEAD_DIM), per_b),
        scratch_shapes=[pltpu.VMEM((rows, 1), F32), pltpu.VMEM((rows, 1), F32),
                        pltpu.VMEM((rows, HEAD_DIM), F32), pltpu.VMEM((1, w), F32)],
    )
    return pl.pallas_call(
        functools.partial(_fox_sample_kernel, pps=pps),
        grid_spec=grid_spec,
        out_shape=jax.ShapeDtypeStruct((n_batch, rows, HEAD_DIM), F32),
        compiler_params=_params("parallel", "arbitrary"),
        name=name,
    )(page_table.reshape(-1), q_rows, head_mask, new_mask, k_new, v_new, bias_new,
      *([k_pool] * pps), *([v_pool] * pps), *([suf] * pps), *([tot] * pps))


def _rope_kernel(x_ref, c_ref, s_ref, o_ref):
    for hh in range(x_ref.shape[1] // HEAD_DIM):
        sl = slice(hh * HEAD_DIM, (hh + 1) * HEAD_DIM)
        x = x_ref[:, sl]
        o_ref[:, sl] = x * c_ref[...] + pltpu.roll(x, HEAD_DIM // 2, 1) * s_ref[...]


def _rope(p, cos2, sin2, width, name):
    m = p.shape[0]
    tm = _pick_tile(m, ROW_TILE_CAP, 8)
    tc = _pick_tile(width, 512, HEAD_DIM)
    return pl.pallas_call(
        _rope_kernel,
        grid=(m // tm, width // tc),
        in_specs=[pl.BlockSpec((tm, tc), lambda i, j: (i, j)),
                  pl.BlockSpec((tm, HEAD_DIM), lambda i, j: (i, 0)), pl.BlockSpec((tm, HEAD_DIM), lambda i, j: (i, 0))],
        out_specs=pl.BlockSpec((tm, tc), lambda i, j: (i, j)),
        out_shape=jax.ShapeDtypeStruct((m, width), F32),
        compiler_params=_params("parallel", "arbitrary"),
        name=name,
    )(p, cos2, sin2)


def _dot_f32_nt(a, b):
    a3, b3 = _split3(a), _split3(b)
    acc = None
    for ia, ap in enumerate(a3):
        for ib, bp in enumerate(b3):
            if ia + ib > 2:
                continue
            term = lax.dot_general(ap, bp, NT_DIMS, preferred_element_type=F32)
            acc = term if acc is None else acc + term
    return acc


def _topk_mask(gate, n_blocks, topk):
    lane = lax.broadcasted_iota(jnp.int32, gate.shape, 1)
    cnt = jnp.zeros_like(gate)
    for mth in range(n_blocks):
        gm = gate[:, mth:mth + 1]
        ahead = (gm > gate) | ((gm == gate) & (mth < lane))
        cnt = cnt + jnp.where(ahead, 1.0, 0.0)
    return jnp.where((cnt < topk) & (gate > -jnp.inf), 1.0, 0.0)


def _moba_prompt_kernel(q_ref, k_ref, v_ref, o_ref, km_ref, *, blk):
    qi = pl.program_id(2)
    nb = k_ref.shape[0] // blk
    group = q_ref.shape[1] // HEAD_DIM

    @pl.when(qi == 0)
    def _():
        km_ref[...] = jnp.zeros_like(km_ref)
        for n in range(nb):
            km_ref[n:n + 1, :] = jnp.mean(k_ref[n * blk:(n + 1) * blk, :], axis=0, keepdims=True)

    heads = [q_ref[:, r * HEAD_DIM:(r + 1) * HEAD_DIM] for r in range(group)]
    q_sum = heads[0]
    for r in range(1, group):
        q_sum = q_sum + heads[r]
    gate = _dot_f32_nt(q_sum, km_ref[...])
    lane = lax.broadcasted_iota(jnp.int32, gate.shape, 1)
    gate = jnp.where(lane < qi, gate, -jnp.inf)
    drop = (1.0 - _topk_mask(gate, nb - 1, MOBA_TOPK)) * NEG

    for nk in range(1, nb + 1):
        @pl.when(qi == nk - 1)
        def _(nk=nk):
            w = nk * blk
            kb = k_ref[0:w, :].astype(BF16)
            vb = v_ref[0:w, :].astype(BF16)
            bias = [jnp.broadcast_to(drop[:, n:n + 1], (blk, blk)) for n in range(nk - 1)]
            for r in range(group):
                s = lax.dot_general(heads[r].astype(BF16), kb, NT_DIMS, preferred_element_type=F32) * ATTN_SCALE
                s = _causal_tail(s, blk)
                if nk > 1:
                    s = jnp.concatenate([s[:, n * blk:(n + 1) * blk] + bias[n] for n in range(nk - 1)]
                                        + [s[:, w - blk:]], axis=1)
                o_ref[:, r * HEAD_DIM:(r + 1) * HEAD_DIM] = _softmax_pv(s, vb).astype(o_ref.dtype)


def _moba_prompt(qk, p, n_batch, t, kv_heads, col_v, name):
    blk = MOBA_BLOCK
    gw = MOBA_GROUP * HEAD_DIM
    ck = kv_heads * MOBA_GROUP
    cv = col_v // HEAD_DIM
    assert t // blk <= LANES
    return pl.pallas_call(
        functools.partial(_moba_prompt_kernel, blk=blk),
        grid=(n_batch, kv_heads, t // blk),
        in_specs=[pl.BlockSpec((blk, gw), lambda b, g, i: (b * (t // blk) + i, g)),
                  pl.BlockSpec((t, HEAD_DIM), lambda b, g, i: (b, ck + g)),
                  pl.BlockSpec((t, HEAD_DIM), lambda b, g, i: (b, cv + g))],
        out_specs=pl.BlockSpec((blk, gw), lambda b, g, i: (b * (t // blk) + i, g)),
        out_shape=jax.ShapeDtypeStruct((n_batch * t, kv_heads * gw), BF16),
        scratch_shapes=[pltpu.VMEM((LANES, HEAD_DIM), F32)],
        compiler_params=_params("parallel", "parallel", "arbitrary"),
        name=name,
    )(qk, qk, p)


def _kmean_kernel(pt_ref, *refs, n_in, pages_per_block, kv_heads):
    del pt_ref
    o_ref = refs[n_in]
    blocks = n_in // pages_per_block
    for j in range(blocks):
        tot = None
        for w in range(pages_per_block):
            x_ref = refs[j * pages_per_block + w]
            for tok in range(PAGE_SIZE):
                part = x_ref[tok * kv_heads:(tok + 1) * kv_heads, :]
                tot = part if tot is None else tot + part
        o_ref[0, pl.program_id(1) * blocks + j] = tot / (pages_per_block * PAGE_SIZE)


def _moba_kmean(page_table, k_pool, page0, kv_heads, name):
    n_batch, n_pages = page_table.shape
    ppb = MOBA_BLOCK // PAGE_SIZE
    nb = n_pages // ppb
    blocks = _pick_tile(nb, 4, 1)
    n_in = blocks * ppb
    w = PAGE_SIZE * kv_heads

    def page(slot):
        return lambda b, n, pt: (page0 + pt[b * n_pages + n * n_in + slot], 0)

    grid_spec = pltpu.PrefetchScalarGridSpec(
        num_scalar_prefetch=1,
        grid=(n_batch, nb // blocks),
        in_specs=[pl.BlockSpec((w, HEAD_DIM), page(s)) for s in range(n_in)],
        out_specs=pl.BlockSpec((1, nb, kv_heads, HEAD_DIM), lambda b, n, pt: (b, 0, 0, 0)),
    )
    return pl.pallas_call(
        functools.partial(_kmean_kernel, n_in=n_in, pages_per_block=ppb, kv_heads=kv_heads),
        grid_spec=grid_spec,
        out_shape=jax.ShapeDtypeStruct((n_batch, nb, kv_heads, HEAD_DIM), F32),
        compiler_params=_params("parallel", "arbitrary"),
        name=name,
    )(page_table.reshape(-1), *([k_pool] * n_in))


def _moba_sample_kernel(pt_ref, q_ref, qr_ref, hm_ref, nm_ref, km_ref, kn_ref, vn_ref, *refs,
                        pps, kv_heads, n_new):
    del pt_ref
    k_refs, v_refs = refs[:pps], refs[pps:2 * pps]
    o_ref, m_ref, l_ref, acc_ref, sel_ref = refs[2 * pps:]
    pg = pl.program_id(1)
    q_rows = qr_ref[0]
    rows_per_head = MOBA_GROUP * n_new
    nb = km_ref.shape[2]
    ppb = MOBA_BLOCK // PAGE_SIZE

    @pl.when(pg == 0)
    def _():
        q = q_ref[0]
        for g in range(kv_heads):
            q_sum = q[:, g * MOBA_GROUP * HEAD_DIM:(g * MOBA_GROUP + 1) * HEAD_DIM]
            for r in range(1, MOBA_GROUP):
                c0 = (g * MOBA_GROUP + r) * HEAD_DIM
                q_sum = q_sum + q[:, c0:c0 + HEAD_DIM]
            sel = _topk_mask(_dot_f32_nt(q_sum, km_ref[0, g]), nb, MOBA_TOPK)
            for r in range(MOBA_GROUP):
                r0 = g * rows_per_head + r * n_new
                sel_ref[r0:r0 + n_new, 0:nb] = sel
        s = _scores(q_rows, kn_ref.at[0]) + nm_ref[...]
        _online_update([s], [vn_ref[0].astype(BF16)], m_ref, l_ref, acc_ref, first=True)

    lane = lax.broadcasted_iota(jnp.int32, (q_rows.shape[0], nb), 1)
    s_parts, v_parts = [], []
    for w in range(pps):
        if w % ppb == 0:
            blk = (pg * pps + w) // ppb
            keep = jnp.sum(jnp.where(lane == blk, sel_ref[:, 0:nb], 0.0), axis=1, keepdims=True)
            drop = (1.0 - keep) * NEG
        s_parts.append(_scores(q_rows, k_refs[w]) + drop + hm_ref[...])
        v_parts.append(v_refs[w][...].astype(BF16))
    _online_update(s_parts, v_parts, m_ref, l_ref, acc_ref, first=False)

    @pl.when(pg == pl.num_programs(1) - 1)
    def _():
        o_ref[0] = acc_ref[...] / l_ref[...]


def _moba_sample(page_table, q, q_rows, head_mask, new_mask, k_mean, k_new, v_new, k_pool, v_pool, page0, pps,
                 kv_heads, n_new, name):
    n_batch, n_pages = page_table.shape
    rows = q_rows.shape[1]
    w = head_mask.shape[1]
    wn = new_mask.shape[1]
    nb = k_mean.shape[2]
    assert nb <= LANES and pps % (MOBA_BLOCK // PAGE_SIZE) == 0
    per_b = lambda b, p, pt: (b, 0, 0)
    const = lambda b, p, pt: (0, 0)

    def page(slot):
        return lambda b, p, pt: (page0 + pt[b * n_pages + p * pps + slot], 0)

    grid_spec = pltpu.PrefetchScalarGridSpec(
        num_scalar_prefetch=1,
        grid=(n_batch, n_pages // pps),
        in_specs=[pl.BlockSpec((1, n_new, q.shape[2]), per_b), pl.BlockSpec((1, rows, HEAD_DIM), per_b),
                  pl.BlockSpec((rows, w), const), pl.BlockSpec((rows, wn), const),
                  pl.BlockSpec((1, kv_heads, nb, HEAD_DIM), lambda b, p, pt: (b, 0, 0, 0)),
                  pl.BlockSpec((1, wn, HEAD_DIM), per_b), pl.BlockSpec((1, wn, HEAD_DIM), per_b)]
                 + [pl.BlockSpec((w, HEAD_DIM), page(s)) for s in range(pps)] * 2,
        out_specs=pl.BlockSpec((1, rows, HEAD_DIM), per_b),
        scratch_shapes=[pltpu.VMEM((rows, 1), F32), pltpu.VMEM((rows, 1), F32),
                        pltpu.VMEM((rows, HEAD_DIM), F32), pltpu.VMEM((rows, LANES), F32)],
    )
    return pl.pallas_call(
        functools.partial(_moba_sample_kernel, pps=pps, kv_heads=kv_heads, n_new=n_new),
        grid_spec=grid_spec,
        out_shape=jax.ShapeDtypeStruct((n_batch, rows, HEAD_DIM), F32),
        compiler_params=_params("parallel", "arbitrary"),
        name=name,
    )(page_table.reshape(-1), q, q_rows, head_mask, new_mask, k_mean, k_new, v_new,
      *([k_pool] * pps), *([v_pool] * pps))


def _cross_kernel(q_ref, k_ref, v_ref, o_ref):
    s = lax.dot_general(q_ref[...].astype(BF16), k_ref[...].astype(BF16), NT_DIMS, preferred_element_type=F32) * CROSS_SCALE
    m = jnp.max(s, axis=1, keepdims=True)
    p = jnp.exp(s - m)
    l = jnp.sum(p, axis=1, keepdims=True)
    o = jnp.dot(p.astype(BF16), v_ref[...].astype(BF16), preferred_element_type=F32)
    o_ref[...] = (o / l).astype(o_ref.dtype)


def _cross(q, mem_k, mem_v, row0, n_batch, t, out_dtype, name):
    mem = mem_k.shape[0] // n_batch
    tq = _pick_tile(t, 512, 8)
    nq = t // tq
    r0 = row0 // tq
    return pl.pallas_call(
        _cross_kernel,
        grid=(n_batch, nq, CROSS_HEADS),
        in_specs=[pl.BlockSpec((tq, CROSS_HEAD_DIM), lambda b, i, h: (r0 + b * nq + i, h)),
                  pl.BlockSpec((mem, CROSS_HEAD_DIM), lambda b, i, h: (b, h)),
                  pl.BlockSpec((mem, CROSS_HEAD_DIM), lambda b, i, h: (b, h))],
        out_specs=pl.BlockSpec((tq, CROSS_HEAD_DIM), lambda b, i, h: (b * nq + i, h)),
        out_shape=jax.ShapeDtypeStruct((n_batch * t, CROSS_HEADS * CROSS_HEAD_DIM), out_dtype),
        compiler_params=_params("parallel", "parallel", "parallel"),
        name=name,
    )(q, mem_k, mem_v)


def _head_rows(q):
    nb, t, w = q.shape
    return q.reshape(nb, t, w // HEAD_DIM, HEAD_DIM).transpose(0, 2, 1, 3).reshape(nb, -1, HEAD_DIM).astype(BF16)


def _decode_masks(q_heads, kv_heads, n_new):
    row = jnp.arange(q_heads * n_new)
    own = ((row // n_new) // (q_heads // kv_heads))[:, None]
    t_q = (row % n_new)[:, None]
    lane = jnp.arange(PAGE_SIZE * kv_heads)[None, :]
    head_mask = jnp.where(lane % kv_heads == own, 0.0, NEG).astype(F32)
    lane = jnp.arange(n_new * kv_heads)[None, :]
    new_mask = jnp.where((lane % kv_heads == own) & (lane // kv_heads <= t_q), 0.0, NEG).astype(F32)
    return head_mask, new_mask


def kernel(x_prompt, x_sample, cache_fox_k, cache_fox_v, cache_fox_logf, state_conv, cache_moba_k, cache_moba_v, cache_mem_k, cache_mem_v, page_table, mem_prompt, w_in_even, b_forget, w_conv, w_out_even, w_in_odd, w_out_odd, w_cross_q, w_cross_k, w_cross_v, w_cross_o, w_ffa_gate, w_ffa_up, w_ffa_down, w_ffb_gate, w_ffb_up, w_ffb_down, ln_g, ln_b):
    bp, t, d = x_prompt.shape
    bs, ts, _ = x_sample.shape
    depth = ln_g.shape[0]
    mp, ms = bp * t, bs * ts
    m = mp + ms
    alpha = (2 * depth) ** 0.25
    conv_ch = d // 2
    fox_heads = conv_ch // HEAD_DIM
    fox_w = fox_heads * HEAD_DIM
    moba_heads = d // HEAD_DIM
    kv_heads = moba_heads // MOBA_GROUP
    kv_w = kv_heads * HEAD_DIM
    cross_w = CROSS_HEADS * CROSS_HEAD_DIM
    mem_len = mem_prompt.shape[1]
    n_pages = page_table.shape[1]
    past_len = n_pages * PAGE_SIZE
    new_pad = BF16_SUBLANES

    wb = {name: _to_bf16(w, "cast_" + name) for name, w in (
        ("in_even", w_in_even), ("out_even", w_out_even), ("in_odd", w_in_odd), ("out_odd", w_out_odd),
        ("cross_q", w_cross_q), ("cross_k", w_cross_k), ("cross_v", w_cross_v), ("cross_o", w_cross_o),
        ("ffa_gate", w_ffa_gate), ("ffa_up", w_ffa_up), ("ffa_down", w_ffa_down),
        ("ffb_gate", w_ffb_gate), ("ffb_up", w_ffb_up), ("ffb_down", w_ffb_down))}
    d_ff = w_ffa_gate.shape[2]

    def ffn(act, which, layer, g, b, name):
        hidden = _proj(act, (wb[which + "_gate"], wb[which + "_up"]), layer, d_ff, BF16, name + "_swiglu")
        return _resid(hidden, wb[which + "_down"], layer, act, g, b, alpha, 0.5, name + "_down", tn_cap=256)

    pos = jnp.concatenate([jnp.tile(jnp.arange(t, dtype=jnp.int32), bp),
                           jnp.tile(past_len + jnp.arange(ts, dtype=jnp.int32), bs)])
    half = HEAD_DIM // 2
    inv_freq = ROPE_THETA ** (-jnp.arange(half, dtype=F32) / half)
    ang = pos.astype(F32)[:, None] * inv_freq[None, :]
    cos2 = jnp.concatenate([jnp.cos(ang), jnp.cos(ang)], axis=1)
    sin2 = jnp.concatenate([-jnp.sin(ang), jnp.sin(ang)], axis=1)

    ones = jnp.ones((1, d), F32)
    zeros = jnp.zeros((1, d), F32)
    unit_stats = lambda rows: jnp.zeros((rows, LANES), F32).at[:, 1].set(1.0)

    mem_act = (mem_prompt.reshape(bp * mem_len, d), unit_stats(bp * mem_len), ones, zeros)
    mem_k_p = jnp.stack([_proj(mem_act, (wb["cross_k"],), i, cross_w, F32, f"mem_k{i}") for i in range(depth)])
    mem_v_p = jnp.stack([_proj(mem_act, (wb["cross_v"],), i, cross_w, F32, f"mem_v{i}") for i in range(depth)])

    x0 = jnp.concatenate([x_prompt.reshape(mp, d), x_sample.reshape(ms, d)], axis=0)
    act = (x0, unit_stats(m), ones, zeros)
    st_p = {name: [] for name in ("fox_k", "fox_v", "fox_logf", "conv", "moba_k", "moba_v")}
    st_s = {name: [] for name in st_p}

    for i in range(depth):
        g = [ln_g[i, s].reshape(1, d) for s in range(4)]
        b = [ln_b[i, s].reshape(1, d) for s in range(4)]
        j = i // 2
        act = ffn(act, "ffa", i, g[0], b[0], f"ffa{i}")

        if i % 2 == 0:
            n_main = 3 * conv_ch + 3 * fox_w
            p = _proj(act, (wb["in_even"],), j, n_main, F32, f"in_even{i}")
            w_f = jnp.pad(w_in_even[j, :, n_main:].astype(BF16), ((0, 0), (0, LANES - fox_heads)))[None]
            f = _proj(act, (w_f,), 0, LANES, F32, f"in_forget{i}")
            b_f = jnp.pad(b_forget[j].astype(F32), (0, LANES - fox_heads)).reshape(1, LANES)
            col_q, col_k, col_v = 3 * conv_ch, 3 * conv_ch + fox_w, 3 * conv_ch + 2 * fox_w

            a_p, conv_p = _conv(p, jnp.zeros((bp, CONV_WIDTH - 1, conv_ch), F32), w_conv[j], 0, bp, t, conv_ch,
                                BF16, f"conv_p{i}")
            a_s, conv_s = _conv(p, state_conv[j], w_conv[j], mp, bs, ts, conv_ch, F32, f"conv_s{i}")
            a_s = a_s.astype(BF16)

            lf_p, nc_p = _logf(f, b_f, 0, bp, t, f"logf_p{i}")
            lf_s, nc_s = _logf(f, b_f, mp, bs, ts, f"logf_s{i}")
            tq = _pick_tile(t, 256, LANES)
            nct = nc_p[:, :fox_heads].reshape(bp, t, fox_heads).transpose(0, 2, 1).reshape(bp, fox_heads, 1, t)
            o_p = _fox_prompt(p, nct, bp, t, fox_heads, col_q, tq, f"fox_p{i}")

            ps = p[mp:]
            q_s = ps[:, col_q:col_q + fox_w].reshape(bs, ts, fox_w)
            k_s = ps[:, col_k:col_k + fox_w].reshape(bs, ts, fox_w)
            v_s = ps[:, col_v:col_v + fox_w].reshape(bs, ts, fox_w)
            head_mask, new_mask = _decode_masks(fox_heads, fox_heads, ts)
            bias_new = nc_s[:, :fox_heads].reshape(bs, 1, ts * fox_heads)
            n_phys = cache_fox_k.shape[1]
            lf_suffix, lf_total = _lf_scan(cache_fox_logf[j].reshape(n_phys, PAGE_SIZE * fox_heads), fox_heads,
                                           f"lf_scan{i}")
            o_s = _fox_sample(page_table, _head_rows(q_s), head_mask, new_mask,
                              k_s.reshape(bs, ts * fox_heads, HEAD_DIM), v_s.reshape(bs, ts * fox_heads, HEAD_DIM),
                              bias_new, cache_fox_k.reshape(-1, HEAD_DIM), cache_fox_v.reshape(-1, HEAD_DIM),
                              lf_suffix.reshape(n_phys, 1, -1), lf_total.reshape(n_phys, 1, -1),
                              j * n_phys, FOX_PAGES_PER_STEP, f"fox_s{i}")
            o_s = o_s.reshape(bs, fox_heads, ts, HEAD_DIM).transpose(0, 2, 1, 3).reshape(ms, fox_w).astype(BF16)

            mix_in = jnp.concatenate([jnp.concatenate([a_p, o_p], axis=1), jnp.concatenate([a_s, o_s], axis=1)], axis=0)
            w_out = wb["out_even"]

            st_p["fox_k"].append(p[:mp, col_k:col_k + fox_w].reshape(bp, t, fox_heads, HEAD_DIM))
            st_p["fox_v"].append(p[:mp, col_v:col_v + fox_w].reshape(bp, t, fox_heads, HEAD_DIM))
            st_p["fox_logf"].append(lf_p[:, :fox_heads].reshape(bp, t, fox_heads))
            st_p["conv"].append(conv_p)
            st_s["fox_k"].append(k_s.reshape(bs, ts, fox_heads, HEAD_DIM))
            st_s["fox_v"].append(v_s.reshape(bs, ts, fox_heads, HEAD_DIM))
            st_s["fox_logf"].append(lf_s[:, :fox_heads].reshape(bs, ts, fox_heads))
            st_s["conv"].append(conv_s)
        else:
            p = _proj(act, (wb["in_odd"],), j, d + 2 * kv_w, F32, f"in_odd{i}")
            qk_w = d + kv_w
            qk = _rope(p, cos2, sin2, qk_w, f"rope{i}")
            o_p = _moba_prompt(qk, p, bp, t, kv_heads, qk_w, f"moba_p{i}")

            q_s = qk[mp:, :d].reshape(bs, ts, d)
            k_s = qk[mp:, d:].reshape(bs, ts, kv_w)
            v_s = p[mp:, qk_w:].reshape(bs, ts, kv_w)
            head_mask, new_mask = _decode_masks(moba_heads, kv_heads, ts)
            n_phys = cache_moba_k.shape[1]
            k_pool = cache_moba_k.reshape(-1, HEAD_DIM)
            v_pool = cache_moba_v.reshape(-1, HEAD_DIM)
            k_mean = _moba_kmean(page_table, k_pool, j * n_phys, kv_heads, f"moba_kmean{i}").transpose(0, 2, 1, 3)
            o_s = _moba_sample(page_table, q_s, _head_rows(q_s), head_mask, new_mask, k_mean,
                               k_s.reshape(bs, ts * kv_heads, HEAD_DIM), v_s.reshape(bs, ts * kv_heads, HEAD_DIM),
                               k_pool, v_pool, j * n_phys, MOBA_PAGES_PER_STEP, kv_heads, ts, f"moba_s{i}")
            o_s = o_s.reshape(bs, moba_heads, ts, HEAD_DIM).transpose(0, 2, 1, 3).reshape(ms, d).astype(BF16)

            mix_in = jnp.concatenate([o_p, o_s], axis=0)
            w_out = wb["out_odd"]

            st_p["moba_k"].append(qk[:mp, d:].reshape(bp, t, kv_heads, HEAD_DIM))
            st_p["moba_v"].append(p[:mp, qk_w:].reshape(bp, t, kv_heads, HEAD_DIM))
            st_s["moba_k"].append(k_s.reshape(bs, ts, kv_heads, HEAD_DIM))
            st_s["moba_v"].append(v_s.reshape(bs, ts, kv_heads, HEAD_DIM))

        act = _resid(mix_in, w_out, j, act, g[1], b[1], alpha, 1.0, f"mix_out{i}")

        q = _proj(act, (wb["cross_q"],), i, cross_w, F32, f"cross_q{i}")
        o_p = _cross(q, mem_k_p[i], mem_v_p[i], 0, bp, t, BF16, f"cross_p{i}")
        o_s = _cross(q, cache_mem_k[i].reshape(bs * mem_len, cross_w), cache_mem_v[i].reshape(bs * mem_len, cross_w),
                     mp, bs, ts, F32, f"cross_s{i}").astype(BF16)
        act = _resid(jnp.concatenate([o_p, o_s], axis=0), wb["cross_o"], i, act, g[2], b[2], alpha, 1.0,
                     f"cross_o{i}")

        act = ffn(act, "ffb", i, g[3], b[3], f"ffb{i}")

    y_prompt = _final_norm(act, 0, mp, "y_prompt").reshape(bp, t, d)
    y_sample = _final_norm(act, mp, ms, "y_sample").reshape(bs, ts, d)
    stack = lambda xs: jnp.stack(xs)
    mem_shape = (depth, bp, mem_len, CROSS_HEADS, CROSS_HEAD_DIM)
    return (y_prompt, y_sample,
            stack(st_p["fox_k"]), stack(st_p["fox_v"]), stack(st_p["fox_logf"]), stack(st_p["conv"]),
            stack(st_p["moba_k"]), stack(st_p["moba_v"]),
            mem_k_p.reshape(mem_shape), mem_v_p.reshape(mem_shape),
            stack(st_s["fox_k"]), stack(st_s["fox_v"]), stack(st_s["fox_logf"]), stack(st_s["conv"]),
            stack(st_s["moba_k"]), stack(st_s["moba_v"]))
```

```python
import functools
import math

import jax
import jax.numpy as jnp
from jax import lax
from jax.experimental import pallas as pl
from jax.experimental.pallas import tpu as pltpu

F32 = jnp.float32
BF16 = jnp.bfloat16

HEAD_DIM = 128
CONV_WIDTH = 3
PAGE_SIZE = 128
MOBA_BLOCK = 256
MOBA_TOPK = 3
MOBA_GROUP = 4
CROSS_HEADS = 4
CROSS_HEAD_DIM = 256
ROPE_THETA = 10000.0
LN_EPS = 1e-5
ATTN_SCALE = HEAD_DIM ** -0.5
CROSS_SCALE = CROSS_HEAD_DIM ** -0.5

LANES = 128
BF16_SUBLANES = 16
VMEM_LIMIT_BYTES = 56 * 1024 * 1024
ROW_TILE_CAP = 704
COL_TILE_CAP = 512
SWIGLU_TILE_COLS = 256
SWIGLU_TILES = 2
LOG2E = 1.4426950408889634
CAST_BLOCK_BYTES = 8 * 1024 * 1024
FOX_PAGES_PER_STEP = 4
MOBA_PAGES_PER_STEP = 8
NEG = -1e30
NT_DIMS = (((1,), (1,)), ((), ()))


def _pick_tile(n, cap, mult):
    best = None
    for d in range(mult, min(n, cap) + 1, mult):
        if n % d == 0:
            best = d
    if best is None:
        raise ValueError(f"no tile for {n} (cap {cap}, multiple of {mult})")
    return best


def _params(*sem):
    return pltpu.CompilerParams(dimension_semantics=sem, vmem_limit_bytes=VMEM_LIMIT_BYTES)


def _normalize(z, st, g, b):
    return (z - st[:, 0:1]) * st[:, 1:2] * g + b


def _split3(x):
    hi = x.astype(BF16)
    r = x - hi.astype(F32)
    mid = r.astype(BF16)
    lo = (r - mid.astype(F32)).astype(BF16)
    return hi, mid, lo


def _dot01(a01, x, dims=(((1,), (0,)), ((), ()))):
    hi, mid, lo = _split3(x)
    acc = lax.dot_general(a01, hi, dims, preferred_element_type=F32)
    acc = acc + lax.dot_general(a01, mid, dims, preferred_element_type=F32)
    return acc + lax.dot_general(a01, lo, dims, preferred_element_type=F32)


def _proj_kernel(z_ref, st_ref, g_ref, b_ref, *rest, swiglu_tiles, has_aux):
    n_main = 2 * swiglu_tiles if swiglu_tiles else 1
    n_in = n_main + has_aux
    w_refs, o_ref, xb_ref = rest[:n_main], rest[n_in], rest[-1]
    if has_aux:
        w_aux_ref, o_aux_ref = rest[n_main], rest[n_in + 1]

    @pl.when(pl.program_id(1) == 0)
    def _():
        xb_ref[...] = _normalize(z_ref[...], st_ref[...], g_ref[...], b_ref[...]).astype(BF16)
        if has_aux:
            o_aux_ref[...] = jnp.dot(xb_ref[...], w_aux_ref[...], preferred_element_type=F32)

    xb = xb_ref[...]
    if swiglu_tiles:
        tn = w_refs[0].shape[1]
        for h in range(swiglu_tiles):
            a = jnp.dot(xb, w_refs[h][...], preferred_element_type=F32)
            c = jnp.dot(xb, w_refs[swiglu_tiles + h][...], preferred_element_type=F32)
            o_ref[:, h * tn:(h + 1) * tn] = (a * (1.0 / (1.0 + jnp.exp(-a))) * c).astype(o_ref.dtype)
    else:
        o_ref[...] = jnp.dot(xb, w_refs[0][...], preferred_element_type=F32).astype(o_ref.dtype)


def _proj(act, w, layer, n, out_dtype, name, w_aux=None):
    z, st, g, b = act
    m, k = z.shape
    tm = _pick_tile(m, ROW_TILE_CAP, BF16_SUBLANES)
    tn = _pick_tile(n, COL_TILE_CAP, LANES)
    row = lambda i, j: (i, 0)
    has_aux = w_aux is not None
    in_specs = [pl.BlockSpec((tm, k), row), pl.BlockSpec((tm, LANES), row),
                pl.BlockSpec((1, k), lambda i, j: (0, 0)), pl.BlockSpec((1, k), lambda i, j: (0, 0)),
                pl.BlockSpec((None, k, tn), lambda i, j: (layer, 0, j))]
    out_specs = [pl.BlockSpec((tm, tn), lambda i, j: (i, j))]
    out_shape = [jax.ShapeDtypeStruct((m, n), out_dtype)]
    operands = [z, st, g, b, w]
    if has_aux:
        in_specs.append(pl.BlockSpec((None, k, LANES), lambda i, j: (0, 0, 0)))
        out_specs.append(pl.BlockSpec((tm, LANES), row))
        out_shape.append(jax.ShapeDtypeStruct((m, LANES), F32))
        operands.append(w_aux)
    out = pl.pallas_call(
        functools.partial(_proj_kernel, swiglu_tiles=0, has_aux=has_aux),
        grid=(m // tm, n // tn),
        in_specs=in_specs,
        out_specs=out_specs,
        out_shape=out_shape,
        scratch_shapes=[pltpu.VMEM((tm, k), BF16)],
        compiler_params=_params("parallel", "arbitrary"),
        name=name,
    )(*operands)
    return out if has_aux else out[0]


def _swiglu(act, w_gate, w_up, layer, name):
    z, st, g, b = act
    m, k = z.shape
    n = w_gate.shape[2]
    tm = _pick_tile(m, ROW_TILE_CAP, BF16_SUBLANES)
    tn = _pick_tile(n, SWIGLU_TILE_COLS, LANES)
    tiles = n // tn
    steps = -(-tiles // SWIGLU_TILES)
    row = lambda i, j: (i, 0)

    def tile(h):
        return lambda i, j: (layer, 0, jnp.minimum(j * SWIGLU_TILES + h, tiles - 1))

    w_specs = [pl.BlockSpec((None, k, tn), tile(h)) for h in range(SWIGLU_TILES)]
    return pl.pallas_call(
        functools.partial(_proj_kernel, swiglu_tiles=SWIGLU_TILES, has_aux=False),
        grid=(m // tm, steps),
        in_specs=[pl.BlockSpec((tm, k), row), pl.BlockSpec((tm, LANES), row),
                  pl.BlockSpec((1, k), lambda i, j: (0, 0)), pl.BlockSpec((1, k), lambda i, j: (0, 0))]
                 + w_specs + w_specs,
        out_specs=pl.BlockSpec((tm, SWIGLU_TILES * tn), lambda i, j: (i, j)),
        out_shape=jax.ShapeDtypeStruct((m, steps * SWIGLU_TILES * tn), BF16),
        scratch_shapes=[pltpu.VMEM((tm, k), BF16)],
        compiler_params=_params("parallel", "arbitrary"),
        name=name,
    )(z, st, g, b, *([w_gate] * SWIGLU_TILES), *([w_up] * SWIGLU_TILES))


def _resid_kernel(a_ref, w_ref, zr_ref, sr_ref, gr_ref, br_ref, zo_ref, so_ref, mean_ref, m2_ref,
                  *, alpha, scale, n_total):
    j = pl.program_id(1)
    nj = n_total // zo_ref.shape[1]
    y = jnp.dot(a_ref[...], w_ref[...], preferred_element_type=F32)
    x_res = _normalize(zr_ref[...], sr_ref[...], gr_ref[...], br_ref[...])
    z_new = alpha * x_res + scale * y
    zo_ref[...] = z_new
    tn = z_new.shape[1]
    mu_t = jnp.mean(z_new, axis=1, keepdims=True)
    d = z_new - mu_t
    m2_t = jnp.sum(d * d, axis=1, keepdims=True)
    lane = lax.broadcasted_iota(jnp.int32, mean_ref.shape, 1)

    @pl.when(j == 0)
    def _():
        mean_ref[...] = jnp.zeros_like(mean_ref)
        m2_ref[...] = jnp.zeros_like(m2_ref)

    mean_ref[...] = jnp.where(lane == j, mu_t, mean_ref[...])
    m2_ref[...] = jnp.where(lane == j, m2_t, m2_ref[...])

    @pl.when(j == nj - 1)
    def _():
        mus = mean_ref[...]
        mean = jnp.sum(mus, axis=1, keepdims=True) / nj
        dev = jnp.where(lane < nj, mus - mean, 0.0)
        m2 = jnp.sum(m2_ref[...], axis=1, keepdims=True) + tn * jnp.sum(dev * dev, axis=1, keepdims=True)
        rstd = lax.rsqrt(m2 / n_total + LN_EPS)
        so_ref[...] = jnp.where(lane == 0, mean, jnp.where(lane == 1, rstd, 0.0))


def _resid(a, w, layer, act, ln_g, ln_b, alpha, scale, name, tn_cap=COL_TILE_CAP):
    z, st, g, b = act
    m = a.shape[0]
    k, n = w.shape[1:]
    tm = _pick_tile(m, ROW_TILE_CAP, BF16_SUBLANES)
    tn = _pick_tile(n, tn_cap, LANES)
    assert n // tn <= LANES
    z_new, st_new = pl.pallas_call(
        functools.partial(_resid_kernel, alpha=alpha, scale=scale, n_total=n),
        grid=(m // tm, n // tn),
        in_specs=[pl.BlockSpec((tm, k), lambda i, j: (i, 0)), pl.BlockSpec((None, k, tn), lambda i, j: (layer, 0, j)),
                  pl.BlockSpec((tm, tn), lambda i, j: (i, j)), pl.BlockSpec((tm, LANES), lambda i, j: (i, 0)),
                  pl.BlockSpec((1, tn), lambda i, j: (0, j)), pl.BlockSpec((1, tn), lambda i, j: (0, j))],
        out_specs=[pl.BlockSpec((tm, tn), lambda i, j: (i, j)), pl.BlockSpec((tm, LANES), lambda i, j: (i, 0))],
        out_shape=[jax.ShapeDtypeStruct((m, n), F32), jax.ShapeDtypeStruct((m, LANES), F32)],
        scratch_shapes=[pltpu.VMEM((tm, LANES), F32), pltpu.VMEM((tm, LANES), F32)],
        compiler_params=_params("parallel", "arbitrary"),
        name=name,
    )(a, w, z, st, g, b)
    return z_new, st_new, ln_g, ln_b


def _cast_kernel(x_ref, o_ref):
    o_ref[...] = x_ref[...].astype(o_ref.dtype)


def _to_bf16(w, name):
    l, k, n = w.shape
    tr = _pick_tile(k, max(BF16_SUBLANES, CAST_BLOCK_BYTES // (4 * n)), BF16_SUBLANES)
    spec = pl.BlockSpec((None, tr, n), lambda i, r: (i, r, 0))
    return pl.pallas_call(
        _cast_kernel,
        grid=(l, k // tr),
        in_specs=[spec],
        out_specs=spec,
        out_shape=jax.ShapeDtypeStruct((l, k, n), BF16),
        compiler_params=_params("parallel", "parallel"),
        name=name,
    )(w)


def _final_norm_kernel(z_ref, st_ref, g_ref, b_ref, o_ref):
    o_ref[...] = _normalize(z_ref[...], st_ref[...], g_ref[...], b_ref[...])


def _final_norm(act, row0, rows, name):
    z, st, g, b = act
    d = z.shape[1]
    tm = _pick_tile(math.gcd(rows, row0) if row0 else rows, 512, 8)
    off = row0 // tm
    return pl.pallas_call(
        _final_norm_kernel,
        grid=(rows // tm,),
        in_specs=[pl.BlockSpec((tm, d), lambda i: (i + off, 0)), pl.BlockSpec((tm, LANES), lambda i: (i + off, 0)),
                  pl.BlockSpec((1, d), lambda i: (0, 0)), pl.BlockSpec((1, d), lambda i: (0, 0))],
        out_specs=pl.BlockSpec((tm, d), lambda i: (i, 0)),
        out_shape=jax.ShapeDtypeStruct((rows, d), F32),
        compiler_params=_params("parallel"),
        name=name,
    )(z, st, g, b)


def _conv_kernel(ain_ref, ab_ref, ac_ref, buf_ref, w_ref, o_ref, nb_ref):
    u = ac_ref[...] * ain_ref[...]
    t = u.shape[0]
    row = lax.broadcasted_iota(jnp.int32, u.shape, 0)
    b0 = buf_ref[0, 0:1, :]
    b1 = buf_ref[0, 1:2, :]
    u1 = jnp.where(row == 0, b1, pltpu.roll(u, 1, 0))
    u2 = jnp.where(row == 0, b0, jnp.where(row == 1, b1, pltpu.roll(u, 2, 0)))
    y = w_ref[0:1, :] * u2 + w_ref[1:2, :] * u1 + w_ref[2:3, :] * u
    o_ref[...] = (ab_ref[...] * y).astype(o_ref.dtype)
    nb_ref[0] = u[t - (CONV_WIDTH - 1):, :]


def _conv(p, buf, w, row0, n_batch, t, ch, out_dtype, name):
    tc = _pick_tile(ch, 512, LANES)
    nc = ch // tc
    r0 = row0 // t
    return pl.pallas_call(
        _conv_kernel,
        grid=(n_batch, nc),
        in_specs=[pl.BlockSpec((t, tc), lambda b, c: (b + r0, c)),
                  pl.BlockSpec((t, tc), lambda b, c: (b + r0, c + nc)),
                  pl.BlockSpec((t, tc), lambda b, c: (b + r0, c + 2 * nc)),
                  pl.BlockSpec((1, CONV_WIDTH - 1, tc), lambda b, c: (b, 0, c)),
                  pl.BlockSpec((CONV_WIDTH, tc), lambda b, c: (0, c))],
        out_specs=[pl.BlockSpec((t, tc), lambda b, c: (b, c)),
                   pl.BlockSpec((1, CONV_WIDTH - 1, tc), lambda b, c: (b, 0, c))],
        out_shape=[jax.ShapeDtypeStruct((n_batch * t, ch), out_dtype),
                   jax.ShapeDtypeStruct((n_batch, CONV_WIDTH - 1, ch), F32)],
        compiler_params=_params("parallel", "parallel"),
        name=name,
    )(p, p, p, buf, w)


def _logf_kernel(f_ref, bf_ref, lf_ref, nc_ref):
    x = f_ref[...] + bf_ref[...]
    lf = jnp.minimum(x, 0.0) - jnp.log1p(jnp.exp(-jnp.abs(x)))
    lf_ref[...] = lf
    row = lax.broadcasted_iota(jnp.int32, lf.shape, 0)
    cs = lf
    shift = 1
    while shift < lf.shape[0]:
        cs = cs + jnp.where(row >= shift, pltpu.roll(cs, shift, 0), 0.0)
        shift *= 2
    nc_ref[...] = -cs


def _logf(f, bias, row0, n_batch, t, name):
    r0 = row0 // t
    return pl.pallas_call(
        _logf_kernel,
        grid=(n_batch,),
        in_specs=[pl.BlockSpec((t, LANES), lambda b: (b + r0, 0)), pl.BlockSpec((1, LANES), lambda b: (0, 0))],
        out_specs=[pl.BlockSpec((t, LANES), lambda b: (b, 0)), pl.BlockSpec((t, LANES), lambda b: (b, 0))],
        out_shape=[jax.ShapeDtypeStruct((n_batch * t, LANES), F32)] * 2,
        compiler_params=_params("parallel"),
        name=name,
    )(f, bias)


def _softmax_pv(s, vb):
    m = jnp.max(s, axis=1, keepdims=True)
    p = jnp.exp2(s - m)
    l = jnp.sum(p, axis=1, keepdims=True)
    return jnp.dot(p.astype(BF16), vb, preferred_element_type=F32) / l


def _causal_tail(s, width):
    row = lax.broadcasted_iota(jnp.int32, (s.shape[0], width), 0)
    col = lax.broadcasted_iota(jnp.int32, (s.shape[0], width), 1)
    tail = jnp.where(col <= row, s[:, s.shape[1] - width:], NEG)
    if s.shape[1] == width:
        return tail
    return jnp.concatenate([s[:, :s.shape[1] - width], tail], axis=1)


def _fox_prompt_kernel(q_ref, k_ref, v_ref, nc_ref, o_ref, *, tq):
    h = pl.program_id(1)
    qi = pl.program_id(2)
    qb = q_ref[...].astype(BF16)
    for nk in range(1, k_ref.shape[0] // tq + 1):
        @pl.when(qi == nk - 1)
        def _(nk=nk):
            w = nk * tq
            kb = k_ref[0:w, :].astype(BF16)
            vb = v_ref[0:w, :].astype(BF16)
            s = (lax.dot_general(qb, kb, NT_DIMS, preferred_element_type=F32) * (ATTN_SCALE * LOG2E)
                 + nc_ref[0, h, :, 0:w] * LOG2E)
            o_ref[...] = _softmax_pv(_causal_tail(s, tq), vb).astype(o_ref.dtype)


def _fox_prompt(p, nct, n_batch, t, heads, col_q, tq, name):
    cq = col_q // HEAD_DIM
    return pl.pallas_call(
        functools.partial(_fox_prompt_kernel, tq=tq),
        grid=(n_batch, heads, t // tq),
        in_specs=[pl.BlockSpec((tq, HEAD_DIM), lambda b, h, i: (b * (t // tq) + i, cq + h)),
                  pl.BlockSpec((t, HEAD_DIM), lambda b, h, i: (b, cq + heads + h)),
                  pl.BlockSpec((t, HEAD_DIM), lambda b, h, i: (b, cq + 2 * heads + h)),
                  pl.BlockSpec((1, heads, 1, t), lambda b, h, i: (b, 0, 0, 0))],
        out_specs=pl.BlockSpec((tq, HEAD_DIM), lambda b, h, i: (b * (t // tq) + i, h)),
        out_shape=jax.ShapeDtypeStruct((n_batch * t, heads * HEAD_DIM), BF16),
        compiler_params=_params("parallel", "parallel", "arbitrary"),
        name=name,
    )(p, p, p, nct)


def _online_update(s_parts, v_parts, m_ref, l_ref, acc_ref, first):
    m_new = jnp.max(s_parts[0], axis=1, keepdims=True)
    for s in s_parts[1:]:
        m_new = jnp.maximum(m_new, jnp.max(s, axis=1, keepdims=True))
    if not first:
        m_new = jnp.maximum(m_ref[...], m_new)
    l_new, pv = None, None
    for s, v in zip(s_parts, v_parts):
        p = jnp.exp2(s - m_new)
        l_part = jnp.sum(p, axis=1, keepdims=True)
        pv_part = jnp.dot(p.astype(BF16), v, preferred_element_type=F32)
        l_new = l_part if l_new is None else l_new + l_part
        pv = pv_part if pv is None else pv + pv_part
    if first:
        l_ref[...] = l_new
        acc_ref[...] = pv
    else:
        corr = jnp.exp2(m_ref[...] - m_new)
        l_ref[...] = l_ref[...] * corr + l_new
        acc_ref[...] = acc_ref[...] * corr + pv
    m_ref[...] = m_new


def _scores(q_rows, k_ref):
    s = lax.dot_general(q_rows, k_ref[...].astype(BF16), NT_DIMS, preferred_element_type=F32)
    return s * (ATTN_SCALE * LOG2E)


def _lf_scan_kernel(lf_ref, suf_ref, tot_ref, *, heads):
    lf = lf_ref[...]
    w = lf.shape[1]
    lane = lax.broadcasted_iota(jnp.int32, lf.shape, 1)
    a = lf
    shift = heads
    while shift < w:
        a = a + jnp.where(lane + shift < w, pltpu.roll(a, w - shift, 1), 0.0)
        shift *= 2
    suf_ref[...] = jnp.where(lane + heads < w, pltpu.roll(a, w - heads, 1), 0.0)
    tot = jnp.where(lane < heads, a, 0.0)
    shift = heads
    while shift < w:
        tot = tot + pltpu.roll(tot, shift, 1)
        shift *= 2
    tot_ref[...] = tot


def _lf_scan(lf2, heads, name):
    n_phys, w = lf2.shape
    tp = _pick_tile(n_phys, 256, 8)
    spec = pl.BlockSpec((tp, w), lambda i: (i, 0))
    return pl.pallas_call(
        functools.partial(_lf_scan_kernel, heads=heads),
        grid=(n_phys // tp,),
        in_specs=[spec],
        out_specs=[spec, spec],
        out_shape=[jax.ShapeDtypeStruct((n_phys, w), F32)] * 2,
        compiler_params=_params("parallel"),
        name=name,
    )(lf2)


def _fox_sample_kernel(pt_ref, q_ref, hm_ref, nm_ref, kn_ref, vn_ref, bn_ref, *refs, pps):
    del pt_ref
    k_refs, v_refs = refs[:pps], refs[pps:2 * pps]
    suf_refs, tot_refs = refs[2 * pps:3 * pps], refs[3 * pps:4 * pps]
    o_ref, m_ref, l_ref, acc_ref, carry_ref = refs[4 * pps:]
    pg = pl.program_id(1)
    q_rows = q_ref[0]

    @pl.when(pg == 0)
    def _():
        s = _scores(q_rows, kn_ref.at[0]) + bn_ref[0] * LOG2E + nm_ref[...]
        _online_update([s], [vn_ref[0].astype(BF16)], m_ref, l_ref, acc_ref, first=True)
        carry_ref[...] = jnp.zeros_like(carry_ref)

    carry = carry_ref[...]
    s_parts, v_parts = [], []
    for w in range(pps):
        s_parts.append(_scores(q_rows, k_refs[w]) + (suf_refs[w][0] + carry) * LOG2E + hm_ref[...])
        v_parts.append(v_refs[w][...].astype(BF16))
        carry = carry + tot_refs[w][0]
    carry_ref[...] = carry
    _online_update(s_parts, v_parts, m_ref, l_ref, acc_ref, first=False)

    @pl.when(pg == pl.num_programs(1) - 1)
    def _():
        o_ref[0] = acc_ref[...] / l_ref[...]


def _fox_sample(page_table, q_rows, head_mask, new_mask, k_new, v_new, bias_new, k_pool, v_pool, suf, tot,
                page0, pps, name):
    n_batch, n_pages = page_table.shape
    rows = q_rows.shape[1]
    w = head_mask.shape[1]
    wn = new_mask.shape[1]
    per_b = lambda b, p, pt: (b, 0, 0)
    const = lambda b, p, pt: (0, 0)

    def page(slot, base):
        return lambda b, p, pt: (base + pt[b * n_pages + n_pages - 1 - (p * pps + slot)], 0)

    def page3(slot):
        return lambda b, p, pt: (pt[b * n_pages + n_pages - 1 - (p * pps + slot)], 0, 0)

    grid_spec = pltpu.PrefetchScalarGridSpec(
        num_scalar_prefetch=1,
        grid=(n_batch, n_pages // pps),
        in_specs=[pl.BlockSpec((1, rows, HEAD_DIM), per_b),
                  pl.BlockSpec((rows, w), const), pl.BlockSpec((rows, wn), const),
                  pl.BlockSpec((1, wn, HEAD_DIM), per_b), pl.BlockSpec((1, wn, HEAD_DIM), per_b),
                  pl.BlockSpec((1, 1, wn), per_b)]
                 + [pl.BlockSpec((w, HEAD_DIM), page(s, page0)) for s in range(pps)] * 2
                 + [pl.BlockSpec((1, 1, w), page3(s)) for s in range(pps)] * 2,
        out_specs=pl.BlockSpec((1, rows, HEAD_DIM), per_b),
        scratch_shapes=[pltpu.VMEM((rows, 1), F32), pltpu.VMEM((rows, 1), F32),
                        pltpu.VMEM((rows, HEAD_DIM), F32), pltpu.VMEM((1, w), F32)],
    )
    return pl.pallas_call(
        functools.partial(_fox_sample_kernel, pps=pps),
        grid_spec=grid_spec,
        out_shape=jax.ShapeDtypeStruct((n_batch, rows, HEAD_DIM), F32),
        compiler_params=_params("parallel", "arbitrary"),
        name=name,
    )(page_table.reshape(-1), q_rows, head_mask, new_mask, k_new, v_new, bias_new,
      *([k_pool] * pps), *([v_pool] * pps), *([suf] * pps), *([tot] * pps))


def _rope_kernel(x_ref, c_ref, s_ref, o_ref):
    for hh in range(x_ref.shape[1] // HEAD_DIM):
        sl = slice(hh * HEAD_DIM, (hh + 1) * HEAD_DIM)
        x = x_ref[:, sl]
        o_ref[:, sl] = x * c_ref[...] + pltpu.roll(x, HEAD_DIM // 2, 1) * s_ref[...]


def _rope(p, cos2, sin2, width, name):
    m = p.shape[0]
    tm = _pick_tile(m, ROW_TILE_CAP, 8)
    tc = _pick_tile(width, 512, HEAD_DIM)
    return pl.pallas_call(
        _rope_kernel,
        grid=(m // tm, width // tc),
        in_specs=[pl.BlockSpec((tm, tc), lambda i, j: (i, j)),
                  pl.BlockSpec((tm, HEAD_DIM), lambda i, j: (i, 0)), pl.BlockSpec((tm, HEAD_DIM), lambda i, j: (i, 0))],
        out_specs=pl.BlockSpec((tm, tc), lambda i, j: (i, j)),
        out_shape=jax.ShapeDtypeStruct((m, width), F32),
        compiler_params=_params("parallel", "arbitrary"),
        name=name,
    )(p, cos2, sin2)


def _dot_f32_nt(a, b):
    a3, b3 = _split3(a), _split3(b)
    acc = None
    for ia, ap in enumerate(a3):
        for ib, bp in enumerate(b3):
            if ia + ib > 2:
                continue
            term = lax.dot_general(ap, bp, NT_DIMS, preferred_element_type=F32)
            acc = term if acc is None else acc + term
    return acc


def _topk_mask(gate, n_blocks, topk):
    lane = lax.broadcasted_iota(jnp.int32, gate.shape, 1)
    cnt = jnp.zeros_like(gate)
    for mth in range(n_blocks):
        gm = gate[:, mth:mth + 1]
        ahead = (gm > gate) | ((gm == gate) & (mth < lane))
        cnt = cnt + jnp.where(ahead, 1.0, 0.0)
    return jnp.where((cnt < topk) & (gate > -jnp.inf), 1.0, 0.0)


def _moba_prompt_kernel(q_ref, k_ref, v_ref, o_ref, km_ref, *, blk):
    qi = pl.program_id(2)
    nb = k_ref.shape[0] // blk
    group = q_ref.shape[1] // HEAD_DIM

    @pl.when(qi == 0)
    def _():
        km_ref[...] = jnp.zeros_like(km_ref)
        for n in range(nb):
            km_ref[n:n + 1, :] = jnp.mean(k_ref[n * blk:(n + 1) * blk, :], axis=0, keepdims=True)

    heads = [q_ref[:, r * HEAD_DIM:(r + 1) * HEAD_DIM] for r in range(group)]
    q_sum = heads[0]
    for r in range(1, group):
        q_sum = q_sum + heads[r]
    gate = _dot_f32_nt(q_sum, km_ref[...])
    lane = lax.broadcasted_iota(jnp.int32, gate.shape, 1)
    gate = jnp.where(lane < qi, gate, -jnp.inf)
    drop = (1.0 - _topk_mask(gate, nb - 1, MOBA_TOPK)) * NEG

    for nk in range(1, nb + 1):
        @pl.when(qi == nk - 1)
        def _(nk=nk):
            w = nk * blk
            kb = k_ref[0:w, :].astype(BF16)
            vb = v_ref[0:w, :].astype(BF16)
            bias = [jnp.broadcast_to(drop[:, n:n + 1], (blk, blk)) for n in range(nk - 1)]
            for r in range(group):
                s = lax.dot_general(heads[r].astype(BF16), kb, NT_DIMS, preferred_element_type=F32)
                s = s * (ATTN_SCALE * LOG2E)
                s = _causal_tail(s, blk)
                if nk > 1:
                    s = jnp.concatenate([s[:, n * blk:(n + 1) * blk] + bias[n] for n in range(nk - 1)]
                                        + [s[:, w - blk:]], axis=1)
                o_ref[:, r * HEAD_DIM:(r + 1) * HEAD_DIM] = _softmax_pv(s, vb).astype(o_ref.dtype)


def _moba_prompt(qk, p, n_batch, t, kv_heads, col_v, name):
    blk = MOBA_BLOCK
    gw = MOBA_GROUP * HEAD_DIM
    ck = kv_heads * MOBA_GROUP
    cv = col_v // HEAD_DIM
    assert t // blk <= LANES
    return pl.pallas_call(
        functools.partial(_moba_prompt_kernel, blk=blk),
        grid=(n_batch, kv_heads, t // blk),
        in_specs=[pl.BlockSpec((blk, gw), lambda b, g, i: (b * (t // blk) + i, g)),
                  pl.BlockSpec((t, HEAD_DIM), lambda b, g, i: (b, ck + g)),
                  pl.BlockSpec((t, HEAD_DIM), lambda b, g, i: (b, cv + g))],
        out_specs=pl.BlockSpec((blk, gw), lambda b, g, i: (b * (t // blk) + i, g)),
        out_shape=jax.ShapeDtypeStruct((n_batch * t, kv_heads * gw), BF16),
        scratch_shapes=[pltpu.VMEM((LANES, HEAD_DIM), F32)],
        compiler_params=_params("parallel", "parallel", "arbitrary"),
        name=name,
    )(qk, qk, p)


def _kmean_kernel(pt_ref, *refs, n_in, pages_per_block, kv_heads):
    del pt_ref
    o_ref = refs[n_in]
    blocks = n_in // pages_per_block
    for j in range(blocks):
        tot = None
        for w in range(pages_per_block):
            x_ref = refs[j * pages_per_block + w]
            for tok in range(PAGE_SIZE):
                part = x_ref[tok * kv_heads:(tok + 1) * kv_heads, :]
                tot = part if tot is None else tot + part
        o_ref[0, pl.program_id(1) * blocks + j] = tot / (pages_per_block * PAGE_SIZE)


def _moba_kmean(page_table, k_pool, page0, kv_heads, name):
    n_batch, n_pages = page_table.shape
    ppb = MOBA_BLOCK // PAGE_SIZE
    nb = n_pages // ppb
    blocks = _pick_tile(nb, 4, 1)
    n_in = blocks * ppb
    w = PAGE_SIZE * kv_heads

    def page(slot):
        return lambda b, n, pt: (page0 + pt[b * n_pages + n * n_in + slot], 0)

    grid_spec = pltpu.PrefetchScalarGridSpec(
        num_scalar_prefetch=1,
        grid=(n_batch, nb // blocks),
        in_specs=[pl.BlockSpec((w, HEAD_DIM), page(s)) for s in range(n_in)],
        out_specs=pl.BlockSpec((1, nb, kv_heads, HEAD_DIM), lambda b, n, pt: (b, 0, 0, 0)),
    )
    return pl.pallas_call(
        functools.partial(_kmean_kernel, n_in=n_in, pages_per_block=ppb, kv_heads=kv_heads),
        grid_spec=grid_spec,
        out_shape=jax.ShapeDtypeStruct((n_batch, nb, kv_heads, HEAD_DIM), F32),
        compiler_params=_params("parallel", "arbitrary"),
        name=name,
    )(page_table.reshape(-1), *([k_pool] * n_in))


def _moba_sample_kernel(pt_ref, q_ref, qr_ref, hm_ref, nm_ref, km_ref, kn_ref, vn_ref, *refs,
                        pps, kv_heads, n_new):
    del pt_ref
    k_refs, v_refs = refs[:pps], refs[pps:2 * pps]
    o_ref, m_ref, l_ref, acc_ref, sel_ref = refs[2 * pps:]
    pg = pl.program_id(1)
    q_rows = qr_ref[0]
    rows_per_head = MOBA_GROUP * n_new
    nb = km_ref.shape[2]
    ppb = MOBA_BLOCK // PAGE_SIZE

    @pl.when(pg == 0)
    def _():
        q = q_ref[0]
        for g in range(kv_heads):
            q_sum = q[:, g * MOBA_GROUP * HEAD_DIM:(g * MOBA_GROUP + 1) * HEAD_DIM]
            for r in range(1, MOBA_GROUP):
                c0 = (g * MOBA_GROUP + r) * HEAD_DIM
                q_sum = q_sum + q[:, c0:c0 + HEAD_DIM]
            sel = _topk_mask(_dot_f32_nt(q_sum, km_ref[0, g]), nb, MOBA_TOPK)
            for r in range(MOBA_GROUP):
                r0 = g * rows_per_head + r * n_new
                sel_ref[r0:r0 + n_new, 0:nb] = sel
        s = _scores(q_rows, kn_ref.at[0]) + nm_ref[...]
        _online_update([s], [vn_ref[0].astype(BF16)], m_ref, l_ref, acc_ref, first=True)

    lane = lax.broadcasted_iota(jnp.int32, (q_rows.shape[0], nb), 1)
    s_parts, v_parts = [], []
    for w in range(pps):
        if w % ppb == 0:
            blk = (pg * pps + w) // ppb
            keep = jnp.sum(jnp.where(lane == blk, sel_ref[:, 0:nb], 0.0), axis=1, keepdims=True)
            drop = (1.0 - keep) * NEG
        s_parts.append(_scores(q_rows, k_refs[w]) + drop + hm_ref[...])
        v_parts.append(v_refs[w][...].astype(BF16))
    _online_update(s_parts, v_parts, m_ref, l_ref, acc_ref, first=False)

    @pl.when(pg == pl.num_programs(1) - 1)
    def _():
        o_ref[0] = acc_ref[...] / l_ref[...]


def _moba_sample(page_table, q, q_rows, head_mask, new_mask, k_mean, k_new, v_new, k_pool, v_pool, page0, pps,
                 kv_heads, n_new, name):
    n_batch, n_pages = page_table.shape
    rows = q_rows.shape[1]
    w = head_mask.shape[1]
    wn = new_mask.shape[1]
    nb = k_mean.shape[2]
    assert nb <= LANES and pps % (MOBA_BLOCK // PAGE_SIZE) == 0
    per_b = lambda b, p, pt: (b, 0, 0)
    const = lambda b, p, pt: (0, 0)

    def page(slot):
        return lambda b, p, pt: (page0 + pt[b * n_pages + p * pps + slot], 0)

    grid_spec = pltpu.PrefetchScalarGridSpec(
        num_scalar_prefetch=1,
        grid=(n_batch, n_pages // pps),
        in_specs=[pl.BlockSpec((1, n_new, q.shape[2]), per_b), pl.BlockSpec((1, rows, HEAD_DIM), per_b),
                  pl.BlockSpec((rows, w), const), pl.BlockSpec((rows, wn), const),
                  pl.BlockSpec((1, kv_heads, nb, HEAD_DIM), lambda b, p, pt: (b, 0, 0, 0)),
                  pl.BlockSpec((1, wn, HEAD_DIM), per_b), pl.BlockSpec((1, wn, HEAD_DIM), per_b)]
                 + [pl.BlockSpec((w, HEAD_DIM), page(s)) for s in range(pps)] * 2,
        out_specs=pl.BlockSpec((1, rows, HEAD_DIM), per_b),
        scratch_shapes=[pltpu.VMEM((rows, 1), F32), pltpu.VMEM((rows, 1), F32),
                        pltpu.VMEM((rows, HEAD_DIM), F32), pltpu.VMEM((rows, LANES), F32)],
    )
    return pl.pallas_call(
        functools.partial(_moba_sample_kernel, pps=pps, kv_heads=kv_heads, n_new=n_new),
        grid_spec=grid_spec,
        out_shape=jax.ShapeDtypeStruct((n_batch, rows, HEAD_DIM), F32),
        compiler_params=_params("parallel", "arbitrary"),
        name=name,
    )(page_table.reshape(-1), q, q_rows, head_mask, new_mask, k_mean, k_new, v_new,
      *([k_pool] * pps), *([v_pool] * pps))


def _cross_kernel(q_ref, k_ref, v_ref, o_ref):
    s = lax.dot_general(q_ref[...].astype(BF16), k_ref[...].astype(BF16), NT_DIMS, preferred_element_type=F32)
    s = s * (CROSS_SCALE * LOG2E)
    m = jnp.max(s, axis=1, keepdims=True)
    p = jnp.exp2(s - m)
    l = jnp.sum(p, axis=1, keepdims=True)
    o = jnp.dot(p.astype(BF16), v_ref[...].astype(BF16), preferred_element_type=F32)
    o_ref[...] = (o / l).astype(o_ref.dtype)


def _cross(q, mem_k, mem_v, row0, n_batch, t, out_dtype, name):
    mem = mem_k.shape[0] // n_batch
    tq = _pick_tile(t, 512, 8)
    nq = t // tq
    r0 = row0 // tq
    return pl.pallas_call(
        _cross_kernel,
        grid=(n_batch, nq, CROSS_HEADS),
        in_specs=[pl.BlockSpec((tq, CROSS_HEAD_DIM), lambda b, i, h: (r0 + b * nq + i, h)),
                  pl.BlockSpec((mem, CROSS_HEAD_DIM), lambda b, i, h: (b, h)),
                  pl.BlockSpec((mem, CROSS_HEAD_DIM), lambda b, i, h: (b, h))],
        out_specs=pl.BlockSpec((tq, CROSS_HEAD_DIM), lambda b, i, h: (b * nq + i, h)),
        out_shape=jax.ShapeDtypeStruct((n_batch * t, CROSS_HEADS * CROSS_HEAD_DIM), out_dtype),
        compiler_params=_params("parallel", "parallel", "parallel"),
        name=name,
    )(q, mem_k, mem_v)


def _head_rows(q):
    nb, t, w = q.shape
    return q.reshape(nb, t, w // HEAD_DIM, HEAD_DIM).transpose(0, 2, 1, 3).reshape(nb, -1, HEAD_DIM).astype(BF16)


def _decode_masks(q_heads, kv_heads, n_new):
    row = jnp.arange(q_heads * n_new)
    own = ((row // n_new) // (q_heads // kv_heads))[:, None]
    t_q = (row % n_new)[:, None]
    lane = jnp.arange(PAGE_SIZE * kv_heads)[None, :]
    head_mask = jnp.where(lane % kv_heads == own, 0.0, NEG).astype(F32)
    lane = jnp.arange(n_new * kv_heads)[None, :]
    new_mask = jnp.where((lane % kv_heads == own) & (lane // kv_heads <= t_q), 0.0, NEG).astype(F32)
    return head_mask, new_mask


def kernel(x_prompt, x_sample, cache_fox_k, cache_fox_v, cache_fox_logf, state_conv, cache_moba_k, cache_moba_v, cache_mem_k, cache_mem_v, page_table, mem_prompt, w_in_even, b_forget, w_conv, w_out_even, w_in_odd, w_out_odd, w_cross_q, w_cross_k, w_cross_v, w_cross_o, w_ffa_gate, w_ffa_up, w_ffa_down, w_ffb_gate, w_ffb_up, w_ffb_down, ln_g, ln_b):
    bp, t, d = x_prompt.shape
    bs, ts, _ = x_sample.shape
    depth = ln_g.shape[0]
    mp, ms = bp * t, bs * ts
    m = mp + ms
    alpha = (2 * depth) ** 0.25
    conv_ch = d // 2
    fox_heads = conv_ch // HEAD_DIM
    fox_w = fox_heads * HEAD_DIM
    moba_heads = d // HEAD_DIM
    kv_heads = moba_heads // MOBA_GROUP
    kv_w = kv_heads * HEAD_DIM
    cross_w = CROSS_HEADS * CROSS_HEAD_DIM
    mem_len = mem_prompt.shape[1]
    n_pages = page_table.shape[1]
    past_len = n_pages * PAGE_SIZE
    new_pad = BF16_SUBLANES

    wb = {name: _to_bf16(w, "cast_" + name) for name, w in (
        ("in_even", w_in_even), ("out_even", w_out_even), ("in_odd", w_in_odd), ("out_odd", w_out_odd),
        ("cross_q", w_cross_q), ("cross_k", w_cross_k), ("cross_v", w_cross_v), ("cross_o", w_cross_o),
        ("ffa_gate", w_ffa_gate), ("ffa_up", w_ffa_up), ("ffa_down", w_ffa_down),
        ("ffb_gate", w_ffb_gate), ("ffb_up", w_ffb_up), ("ffb_down", w_ffb_down))}
    def ffn(act, which, layer, g, b, name):
        hidden = _swiglu(act, wb[which + "_gate"], wb[which + "_up"], layer, name + "_swiglu")
        return _resid(hidden, wb[which + "_down"], layer, act, g, b, alpha, 0.5, name + "_down", tn_cap=256)

    pos = jnp.concatenate([jnp.tile(jnp.arange(t, dtype=jnp.int32), bp),
                           jnp.tile(past_len + jnp.arange(ts, dtype=jnp.int32), bs)])
    half = HEAD_DIM // 2
    inv_freq = ROPE_THETA ** (-jnp.arange(half, dtype=F32) / half)
    ang = pos.astype(F32)[:, None] * inv_freq[None, :]
    cos2 = jnp.concatenate([jnp.cos(ang), jnp.cos(ang)], axis=1)
    sin2 = jnp.concatenate([-jnp.sin(ang), jnp.sin(ang)], axis=1)

    ones = jnp.ones((1, d), F32)
    zeros = jnp.zeros((1, d), F32)
    unit_stats = lambda rows: jnp.zeros((rows, LANES), F32).at[:, 1].set(1.0)

    mem_act = (mem_prompt.reshape(bp * mem_len, d), unit_stats(bp * mem_len), ones, zeros)
    mem_k_p = jnp.stack([_proj(mem_act, wb["cross_k"], i, cross_w, F32, f"mem_k{i}") for i in range(depth)])
    mem_v_p = jnp.stack([_proj(mem_act, wb["cross_v"], i, cross_w, F32, f"mem_v{i}") for i in range(depth)])

    x0 = jnp.concatenate([x_prompt.reshape(mp, d), x_sample.reshape(ms, d)], axis=0)
    act = (x0, unit_stats(m), ones, zeros)
    st_p = {name: [] for name in ("fox_k", "fox_v", "fox_logf", "conv", "moba_k", "moba_v")}
    st_s = {name: [] for name in st_p}

    for i in range(depth):
        g = [ln_g[i, s].reshape(1, d) for s in range(4)]
        b = [ln_b[i, s].reshape(1, d) for s in range(4)]
        j = i // 2
        act = ffn(act, "ffa", i, g[0], b[0], f"ffa{i}")

        if i % 2 == 0:
            n_main = 3 * conv_ch + 3 * fox_w
            w_f = jnp.pad(w_in_even[j, :, n_main:].astype(BF16), ((0, 0), (0, LANES - fox_heads)))[None]
            p, f = _proj(act, wb["in_even"], j, n_main, F32, f"in_even{i}", w_aux=w_f)
            b_f = jnp.pad(b_forget[j].astype(F32), (0, LANES - fox_heads)).reshape(1, LANES)
            col_q, col_k, col_v = 3 * conv_ch, 3 * conv_ch + fox_w, 3 * conv_ch + 2 * fox_w

            a_p, conv_p = _conv(p, jnp.zeros((bp, CONV_WIDTH - 1, conv_ch), F32), w_conv[j], 0, bp, t, conv_ch,
                                BF16, f"conv_p{i}")
            a_s, conv_s = _conv(p, state_conv[j], w_conv[j], mp, bs, ts, conv_ch, F32, f"conv_s{i}")
            a_s = a_s.astype(BF16)

            lf_p, nc_p = _logf(f, b_f, 0, bp, t, f"logf_p{i}")
            lf_s, nc_s = _logf(f, b_f, mp, bs, ts, f"logf_s{i}")
            tq = _pick_tile(t, 256, LANES)
            nct = nc_p[:, :fox_heads].reshape(bp, t, fox_heads).transpose(0, 2, 1).reshape(bp, fox_heads, 1, t)
            o_p = _fox_prompt(p, nct, bp, t, fox_heads, col_q, tq, f"fox_p{i}")

            ps = p[mp:]
            q_s = ps[:, col_q:col_q + fox_w].reshape(bs, ts, fox_w)
            k_s = ps[:, col_k:col_k + fox_w].reshape(bs, ts, fox_w)
            v_s = ps[:, col_v:col_v + fox_w].reshape(bs, ts, fox_w)
            head_mask, new_mask = _decode_masks(fox_heads, fox_heads, ts)
            bias_new = nc_s[:, :fox_heads].reshape(bs, 1, ts * fox_heads)
            n_phys = cache_fox_k.shape[1]
            lf_suffix, lf_total = _lf_scan(cache_fox_logf[j].reshape(n_phys, PAGE_SIZE * fox_heads), fox_heads,
                                           f"lf_scan{i}")
            o_s = _fox_sample(page_table, _head_rows(q_s), head_mask, new_mask,
                              k_s.reshape(bs, ts * fox_heads, HEAD_DIM), v_s.reshape(bs, ts * fox_heads, HEAD_DIM),
                              bias_new, cache_fox_k.reshape(-1, HEAD_DIM), cache_fox_v.reshape(-1, HEAD_DIM),
                              lf_suffix.reshape(n_phys, 1, -1), lf_total.reshape(n_phys, 1, -1),
                              j * n_phys, FOX_PAGES_PER_STEP, f"fox_s{i}")
            o_s = o_s.reshape(bs, fox_heads, ts, HEAD_DIM).transpose(0, 2, 1, 3).reshape(ms, fox_w).astype(BF16)

            mix_in = jnp.concatenate([jnp.concatenate([a_p, o_p], axis=1), jnp.concatenate([a_s, o_s], axis=1)], axis=0)
            w_out = wb["out_even"]

            st_p["fox_k"].append(p[:mp, col_k:col_k + fox_w].reshape(bp, t, fox_heads, HEAD_DIM))
            st_p["fox_v"].append(p[:mp, col_v:col_v + fox_w].reshape(bp, t, fox_heads, HEAD_DIM))
            st_p["fox_logf"].append(lf_p[:, :fox_heads].reshape(bp, t, fox_heads))
            st_p["conv"].append(conv_p)
            st_s["fox_k"].append(k_s.reshape(bs, ts, fox_heads, HEAD_DIM))
            st_s["fox_v"].append(v_s.reshape(bs, ts, fox_heads, HEAD_DIM))
            st_s["fox_logf"].append(lf_s[:, :fox_heads].reshape(bs, ts, fox_heads))
            st_s["conv"].append(conv_s)
        else:
            p = _proj(act, wb["in_odd"], j, d + 2 * kv_w, F32, f"in_odd{i}")
            qk_w = d + kv_w
            qk = _rope(p, cos2, sin2, qk_w, f"rope{i}")
            o_p = _moba_prompt(qk, p, bp, t, kv_heads, qk_w, f"moba_p{i}")

            q_s = qk[mp:, :d].reshape(bs, ts, d)
            k_s = qk[mp:, d:].reshape(bs, ts, kv_w)
            v_s = p[mp:, qk_w:].reshape(bs, ts, kv_w)
            head_mask, new_mask = _decode_masks(moba_heads, kv_heads, ts)
            n_phys = cache_moba_k.shape[1]
            k_pool = cache_moba_k.reshape(-1, HEAD_DIM)
            v_pool = cache_moba_v.reshape(-1, HEAD_DIM)
            k_mean = _moba_kmean(page_table, k_pool, j * n_phys, kv_heads, f"moba_kmean{i}").transpose(0, 2, 1, 3)
            o_s = _moba_sample(page_table, q_s, _head_rows(q_s), head_mask, new_mask, k_mean,
                               k_s.reshape(bs, ts * kv_heads, HEAD_DIM), v_s.reshape(bs, ts * kv_heads, HEAD_DIM),
                               k_pool, v_pool, j * n_phys, MOBA_PAGES_PER_STEP, kv_heads, ts, f"moba_s{i}")
            o_s = o_s.reshape(bs, moba_heads, ts, HEAD_DIM).transpose(0, 2, 1, 3).reshape(ms, d).astype(BF16)

            mix_in = jnp.concatenate([o_p, o_s], axis=0)
            w_out = wb["out_odd"]

            st_p["moba_k"].append(qk[:mp, d:].reshape(bp, t, kv_heads, HEAD_DIM))
            st_p["moba_v"].append(p[:mp, qk_w:].reshape(bp, t, kv_heads, HEAD_DIM))
            st_s["moba_k"].append(k_s.reshape(bs, ts, kv_heads, HEAD_DIM))
            st_s["moba_v"].append(v_s.reshape(bs, ts, kv_heads, HEAD_DIM))

        act = _resid(mix_in, w_out, j, act, g[1], b[1], alpha, 1.0, f"mix_out{i}")

        q = _proj(act, wb["cross_q"], i, cross_w, F32, f"cross_q{i}")
        o_p = _cross(q, mem_k_p[i], mem_v_p[i], 0, bp, t, BF16, f"cross_p{i}")
        o_s = _cross(q, cache_mem_k[i].reshape(bs * mem_len, cross_w), cache_mem_v[i].reshape(bs * mem_len, cross_w),
                     mp, bs, ts, F32, f"cross_s{i}").astype(BF16)
        act = _resid(jnp.concatenate([o_p, o_s], axis=0), wb["cross_o"], i, act, g[2], b[2], alpha, 1.0,
                     f"cross_o{i}")

        act = ffn(act, "ffb", i, g[3], b[3], f"ffb{i}")

    y_prompt = _final_norm(act, 0, mp, "y_prompt").reshape(bp, t, d)
    y_sample = _final_norm(act, mp, ms, "y_sample").reshape(bs, ts, d)
    stack = lambda xs: jnp.stack(xs)
    mem_shape = (depth, bp, mem_len, CROSS_HEADS, CROSS_HEAD_DIM)
    return (y_prompt, y_sample,
            stack(st_p["fox_k"]), stack(st_p["fox_v"]), stack(st_p["fox_logf"]), stack(st_p["conv"]),
            stack(st_p["moba_k"]), stack(st_p["moba_v"]),
            mem_k_p.reshape(mem_shape), mem_v_p.reshape(mem_shape),
            stack(st_s["fox_k"]), stack(st_s["fox_v"]), stack(st_s["fox_logf"]), stack(st_s["conv"]),
            stack(st_s["moba_k"]), stack(st_s["moba_v"]))
```

```python
import functools
import math

import jax
import jax.numpy as jnp
from jax import lax
from jax.experimental import pallas as pl
from jax.experimental.pallas import tpu as pltpu

F32 = jnp.float32
BF16 = jnp.bfloat16

HEAD_DIM = 128
CONV_WIDTH = 3
PAGE_SIZE = 128
MOBA_BLOCK = 256
MOBA_TOPK = 3
MOBA_GROUP = 4
CROSS_HEADS = 4
CROSS_HEAD_DIM = 256
ROPE_THETA = 10000.0
LN_EPS = 1e-5
ATTN_SCALE = HEAD_DIM ** -0.5
CROSS_SCALE = CROSS_HEAD_DIM ** -0.5

LANES = 128
BF16_SUBLANES = 16
VMEM_LIMIT_BYTES = 56 * 1024 * 1024
ROW_TILE_CAP = 704
COL_TILE_CAP = 512
PROJ_COL_TILE_CAP = 512
SWIGLU_TILE_COLS = 256
SWIGLU_TILES = 2
LOG2E = 1.4426950408889634
CAST_BLOCK_BYTES = 16 * 1024 * 1024
FOX_PAGES_PER_STEP = 4
MOBA_PAGES_PER_STEP = 8
NEG = -1e30
NT_DIMS = (((1,), (1,)), ((), ()))


def _pick_tile(n, cap, mult):
    best = None
    for d in range(mult, min(n, cap) + 1, mult):
        if n % d == 0:
            best = d
    if best is None:
        raise ValueError(f"no tile for {n} (cap {cap}, multiple of {mult})")
    return best


def _params(*sem):
    return pltpu.CompilerParams(dimension_semantics=sem, vmem_limit_bytes=VMEM_LIMIT_BYTES)


def _normalize(z, st, g, b):
    return (z - st[:, 0:1]) * st[:, 1:2] * g + b


def _split3(x):
    hi = x.astype(BF16)
    r = x - hi.astype(F32)
    mid = r.astype(BF16)
    lo = (r - mid.astype(F32)).astype(BF16)
    return hi, mid, lo


def _dot01(a01, x, dims=(((1,), (0,)), ((), ()))):
    hi, mid, lo = _split3(x)
    acc = lax.dot_general(a01, hi, dims, preferred_element_type=F32)
    acc = acc + lax.dot_general(a01, mid, dims, preferred_element_type=F32)
    return acc + lax.dot_general(a01, lo, dims, preferred_element_type=F32)


def _proj_kernel(z_ref, st_ref, g_ref, b_ref, *rest, swiglu_tiles, has_aux):
    n_main = 2 * swiglu_tiles if swiglu_tiles else 1
    n_in = n_main + has_aux
    w_refs, o_ref, xb_ref = rest[:n_main], rest[n_in], rest[-1]
    if has_aux:
        w_aux_ref, o_aux_ref = rest[n_main], rest[n_in + 1]

    @pl.when(pl.program_id(1) == 0)
    def _():
        xb_ref[...] = _normalize(z_ref[...], st_ref[...], g_ref[...], b_ref[...]).astype(BF16)
        if has_aux:
            o_aux_ref[...] = jnp.dot(xb_ref[...], w_aux_ref[...], preferred_element_type=F32)

    xb = xb_ref[...]
    if swiglu_tiles:
        tn = w_refs[0].shape[1]
        for h in range(swiglu_tiles):
            a = jnp.dot(xb, w_refs[h][...], preferred_element_type=F32)
            c = jnp.dot(xb, w_refs[swiglu_tiles + h][...], preferred_element_type=F32)
            o_ref[:, h * tn:(h + 1) * tn] = (a * (1.0 / (1.0 + jnp.exp(-a))) * c).astype(o_ref.dtype)
    else:
        o_ref[...] = jnp.dot(xb, w_refs[0][...], preferred_element_type=F32).astype(o_ref.dtype)


def _proj(act, w, layer, n, out_dtype, name, w_aux=None):
    w = w[layer]
    z, st, g, b = act
    m, k = z.shape
    tm = _pick_tile(m, ROW_TILE_CAP, BF16_SUBLANES)
    tn = _pick_tile(n, PROJ_COL_TILE_CAP, LANES)
    row = lambda i, j: (i, 0)
    has_aux = w_aux is not None
    in_specs = [pl.BlockSpec((tm, k), row), pl.BlockSpec((tm, LANES), row),
                pl.BlockSpec((1, k), lambda i, j: (0, 0)), pl.BlockSpec((1, k), lambda i, j: (0, 0)),
                pl.BlockSpec((None, k, tn), lambda i, j: (0, 0, j))]
    out_specs = [pl.BlockSpec((tm, tn), lambda i, j: (i, j))]
    out_shape = [jax.ShapeDtypeStruct((m, n), out_dtype)]
    operands = [z, st, g, b, w]
    if has_aux:
        in_specs.append(pl.BlockSpec((None, k, LANES), lambda i, j: (0, 0, 0)))
        out_specs.append(pl.BlockSpec((tm, LANES), row))
        out_shape.append(jax.ShapeDtypeStruct((m, LANES), F32))
        operands.append(w_aux)
    out = pl.pallas_call(
        functools.partial(_proj_kernel, swiglu_tiles=0, has_aux=has_aux),
        grid=(m // tm, n // tn),
        in_specs=in_specs,
        out_specs=out_specs,
        out_shape=out_shape,
        scratch_shapes=[pltpu.VMEM((tm, k), BF16)],
        compiler_params=_params("parallel", "arbitrary"),
        name=name,
    )(*operands)
    return out if has_aux else out[0]


def _swiglu(act, w_gate, w_up, layer, name):
    w_gate, w_up = w_gate[layer], w_up[layer]
    z, st, g, b = act
    m, k = z.shape
    n = w_gate.shape[2]
    tm = _pick_tile(m, ROW_TILE_CAP, BF16_SUBLANES)
    tn = _pick_tile(n, SWIGLU_TILE_COLS, LANES)
    tiles = n // tn
    steps = -(-tiles // SWIGLU_TILES)
    row = lambda i, j: (i, 0)

    def tile(h):
        return lambda i, j: (0, 0, jnp.minimum(j * SWIGLU_TILES + h, tiles - 1))

    w_specs = [pl.BlockSpec((None, k, tn), tile(h)) for h in range(SWIGLU_TILES)]
    return pl.pallas_call(
        functools.partial(_proj_kernel, swiglu_tiles=SWIGLU_TILES, has_aux=False),
        grid=(m // tm, steps),
        in_specs=[pl.BlockSpec((tm, k), row), pl.BlockSpec((tm, LANES), row),
                  pl.BlockSpec((1, k), lambda i, j: (0, 0)), pl.BlockSpec((1, k), lambda i, j: (0, 0))]
                 + w_specs + w_specs,
        out_specs=pl.BlockSpec((tm, SWIGLU_TILES * tn), lambda i, j: (i, j)),
        out_shape=jax.ShapeDtypeStruct((m, steps * SWIGLU_TILES * tn), BF16),
        scratch_shapes=[pltpu.VMEM((tm, k), BF16)],
        compiler_params=_params("parallel", "arbitrary"),
        name=name,
    )(z, st, g, b, *([w_gate] * SWIGLU_TILES), *([w_up] * SWIGLU_TILES))


def _resid_kernel(a_ref, w_ref, zr_ref, sr_ref, gr_ref, br_ref, zo_ref, so_ref, mean_ref, m2_ref,
                  *, alpha, scale, n_total):
    j = pl.program_id(1)
    nj = n_total // zo_ref.shape[1]
    y = jnp.dot(a_ref[...], w_ref[...], preferred_element_type=F32)
    x_res = _normalize(zr_ref[...], sr_ref[...], gr_ref[...], br_ref[...])
    z_new = alpha * x_res + scale * y
    zo_ref[...] = z_new
    tn = z_new.shape[1]
    mu_t = jnp.mean(z_new, axis=1, keepdims=True)
    d = z_new - mu_t
    m2_t = jnp.sum(d * d, axis=1, keepdims=True)
    lane = lax.broadcasted_iota(jnp.int32, mean_ref.shape, 1)

    @pl.when(j == 0)
    def _():
        mean_ref[...] = jnp.zeros_like(mean_ref)
        m2_ref[...] = jnp.zeros_like(m2_ref)

    mean_ref[...] = jnp.where(lane == j, mu_t, mean_ref[...])
    m2_ref[...] = jnp.where(lane == j, m2_t, m2_ref[...])

    @pl.when(j == nj - 1)
    def _():
        mus = mean_ref[...]
        mean = jnp.sum(mus, axis=1, keepdims=True) / nj
        dev = jnp.where(lane < nj, mus - mean, 0.0)
        m2 = jnp.sum(m2_ref[...], axis=1, keepdims=True) + tn * jnp.sum(dev * dev, axis=1, keepdims=True)
        rstd = lax.rsqrt(m2 / n_total + LN_EPS)
        so_ref[...] = jnp.where(lane == 0, mean, jnp.where(lane == 1, rstd, 0.0))


def _resid(a, w, layer, act, ln_g, ln_b, alpha, scale, name, tn_cap=COL_TILE_CAP):
    w = w[layer]
    z, st, g, b = act
    m = a.shape[0]
    k, n = w.shape[1:]
    tm = _pick_tile(m, ROW_TILE_CAP, BF16_SUBLANES)
    tn = _pick_tile(n, tn_cap, LANES)
    assert n // tn <= LANES
    z_new, st_new = pl.pallas_call(
        functools.partial(_resid_kernel, alpha=alpha, scale=scale, n_total=n),
        grid=(m // tm, n // tn),
        in_specs=[pl.BlockSpec((tm, k), lambda i, j: (i, 0)), pl.BlockSpec((None, k, tn), lambda i, j: (0, 0, j)),
                  pl.BlockSpec((tm, tn), lambda i, j: (i, j)), pl.BlockSpec((tm, LANES), lambda i, j: (i, 0)),
                  pl.BlockSpec((1, tn), lambda i, j: (0, j)), pl.BlockSpec((1, tn), lambda i, j: (0, j))],
        out_specs=[pl.BlockSpec((tm, tn), lambda i, j: (i, j)), pl.BlockSpec((tm, LANES), lambda i, j: (i, 0))],
        out_shape=[jax.ShapeDtypeStruct((m, n), F32), jax.ShapeDtypeStruct((m, LANES), F32)],
        scratch_shapes=[pltpu.VMEM((tm, LANES), F32), pltpu.VMEM((tm, LANES), F32)],
        compiler_params=_params("parallel", "arbitrary"),
        name=name,
    )(a, w, z, st, g, b)
    return z_new, st_new, ln_g, ln_b


def _cast_kernel(*refs):
    n = len(refs) // 2
    for x_ref, o_ref in zip(refs[:n], refs[n:]):
        o_ref[...] = x_ref[...].astype(o_ref.dtype)


def _to_bf16(w, name):
    l, k, n = w.shape
    tr = _pick_tile(k, max(BF16_SUBLANES, CAST_BLOCK_BYTES // (4 * n * l)), BF16_SUBLANES)

    def layer(i):
        return lambda r: (i, r, 0)

    return pl.pallas_call(
        _cast_kernel,
        grid=(k // tr,),
        in_specs=[pl.BlockSpec((1, tr, n), layer(i)) for i in range(l)],
        out_specs=[pl.BlockSpec((1, tr, n), lambda r: (0, r, 0)) for _ in range(l)],
        out_shape=[jax.ShapeDtypeStruct((1, k, n), BF16) for _ in range(l)],
        compiler_params=_params("parallel"),
        name=name,
    )(*([w] * l))


def _final_norm_kernel(z_ref, st_ref, g_ref, b_ref, o_ref):
    o_ref[...] = _normalize(z_ref[...], st_ref[...], g_ref[...], b_ref[...])


def _final_norm(act, row0, rows, name):
    z, st, g, b = act
    d = z.shape[1]
    tm = _pick_tile(math.gcd(rows, row0) if row0 else rows, 512, 8)
    off = row0 // tm
    return pl.pallas_call(
        _final_norm_kernel,
        grid=(rows // tm,),
        in_specs=[pl.BlockSpec((tm, d), lambda i: (i + off, 0)), pl.BlockSpec((tm, LANES), lambda i: (i + off, 0)),
                  pl.BlockSpec((1, d), lambda i: (0, 0)), pl.BlockSpec((1, d), lambda i: (0, 0))],
        out_specs=pl.BlockSpec((tm, d), lambda i: (i, 0)),
        out_shape=jax.ShapeDtypeStruct((rows, d), F32),
        compiler_params=_params("parallel"),
        name=name,
    )(z, st, g, b)


def _conv_kernel(ain_ref, ab_ref, ac_ref, buf_ref, w_ref, o_ref, nb_ref):
    u = ac_ref[...] * ain_ref[...]
    t = u.shape[0]
    row = lax.broadcasted_iota(jnp.int32, u.shape, 0)
    b0 = buf_ref[0, 0:1, :]
    b1 = buf_ref[0, 1:2, :]
    u1 = jnp.where(row == 0, b1, pltpu.roll(u, 1, 0))
    u2 = jnp.where(row == 0, b0, jnp.where(row == 1, b1, pltpu.roll(u, 2, 0)))
    y = w_ref[0:1, :] * u2 + w_ref[1:2, :] * u1 + w_ref[2:3, :] * u
    o_ref[...] = (ab_ref[...] * y).astype(o_ref.dtype)
    nb_ref[0] = u[t - (CONV_WIDTH - 1):, :]


def _conv(p, buf, w, row0, n_batch, t, ch, out_dtype, name):
    tc = _pick_tile(ch, 512, LANES)
    nc = ch // tc
    r0 = row0 // t
    return pl.pallas_call(
        _conv_kernel,
        grid=(n_batch, nc),
        in_specs=[pl.BlockSpec((t, tc), lambda b, c: (b + r0, c)),
                  pl.BlockSpec((t, tc), lambda b, c: (b + r0, c + nc)),
                  pl.BlockSpec((t, tc), lambda b, c: (b + r0, c + 2 * nc)),
                  pl.BlockSpec((1, CONV_WIDTH - 1, tc), lambda b, c: (b, 0, c)),
                  pl.BlockSpec((CONV_WIDTH, tc), lambda b, c: (0, c))],
        out_specs=[pl.BlockSpec((t, tc), lambda b, c: (b, c)),
                   pl.BlockSpec((1, CONV_WIDTH - 1, tc), lambda b, c: (b, 0, c))],
        out_shape=[jax.ShapeDtypeStruct((n_batch * t, ch), out_dtype),
                   jax.ShapeDtypeStruct((n_batch, CONV_WIDTH - 1, ch), F32)],
        compiler_params=_params("parallel", "parallel"),
        name=name,
    )(p, p, p, buf, w)


def _logf_kernel(f_ref, bf_ref, lf_ref, nc_ref):
    x = f_ref[...] + bf_ref[...]
    lf = jnp.minimum(x, 0.0) - jnp.log1p(jnp.exp(-jnp.abs(x)))
    lf_ref[...] = lf
    row = lax.broadcasted_iota(jnp.int32, lf.shape, 0)
    cs = lf
    shift = 1
    while shift < lf.shape[0]:
        cs = cs + jnp.where(row >= shift, pltpu.roll(cs, shift, 0), 0.0)
        shift *= 2
    nc_ref[...] = -cs


def _logf(f, bias, row0, n_batch, t, name):
    r0 = row0 // t
    return pl.pallas_call(
        _logf_kernel,
        grid=(n_batch,),
        in_specs=[pl.BlockSpec((t, LANES), lambda b: (b + r0, 0)), pl.BlockSpec((1, LANES), lambda b: (0, 0))],
        out_specs=[pl.BlockSpec((t, LANES), lambda b: (b, 0)), pl.BlockSpec((t, LANES), lambda b: (b, 0))],
        out_shape=[jax.ShapeDtypeStruct((n_batch * t, LANES), F32)] * 2,
        compiler_params=_params("parallel"),
        name=name,
    )(f, bias)


def _softmax_pv(s, vb):
    m = jnp.max(s, axis=1, keepdims=True)
    p = jnp.exp2(s - m)
    l = jnp.sum(p, axis=1, keepdims=True)
    return jnp.dot(p.astype(BF16), vb, preferred_element_type=F32) / l


def _causal_tail(s, width):
    row = lax.broadcasted_iota(jnp.int32, (s.shape[0], width), 0)
    col = lax.broadcasted_iota(jnp.int32, (s.shape[0], width), 1)
    tail = jnp.where(col <= row, s[:, s.shape[1] - width:], NEG)
    if s.shape[1] == width:
        return tail
    return jnp.concatenate([s[:, :s.shape[1] - width], tail], axis=1)


def _fox_prompt_kernel(q_ref, k_ref, v_ref, nc_ref, o_ref, *, tq):
    h = pl.program_id(1)
    qi = pl.program_id(2)
    qb = q_ref[...].astype(BF16)
    for nk in range(1, k_ref.shape[0] // tq + 1):
        @pl.when(qi == nk - 1)
        def _(nk=nk):
            w = nk * tq
            kb = k_ref[0:w, :].astype(BF16)
            vb = v_ref[0:w, :].astype(BF16)
            s = (lax.dot_general(qb, kb, NT_DIMS, preferred_element_type=F32) * (ATTN_SCALE * LOG2E)
                 + nc_ref[0, h, :, 0:w] * LOG2E)
            o_ref[...] = _softmax_pv(_causal_tail(s, tq), vb).astype(o_ref.dtype)


def _fox_prompt(p, nct, n_batch, t, heads, col_q, tq, name):
    cq = col_q // HEAD_DIM
    return pl.pallas_call(
        functools.partial(_fox_prompt_kernel, tq=tq),
        grid=(n_batch, heads, t // tq),
        in_specs=[pl.BlockSpec((tq, HEAD_DIM), lambda b, h, i: (b * (t // tq) + i, cq + h)),
                  pl.BlockSpec((t, HEAD_DIM), lambda b, h, i: (b, cq + heads + h)),
                  pl.BlockSpec((t, HEAD_DIM), lambda b, h, i: (b, cq + 2 * heads + h)),
                  pl.BlockSpec((1, heads, 1, t), lambda b, h, i: (b, 0, 0, 0))],
        out_specs=pl.BlockSpec((tq, HEAD_DIM), lambda b, h, i: (b * (t // tq) + i, h)),
        out_shape=jax.ShapeDtypeStruct((n_batch * t, heads * HEAD_DIM), BF16),
        compiler_params=_params("parallel", "parallel", "arbitrary"),
        name=name,
    )(p, p, p, nct)


def _online_update(s_parts, v_parts, m_ref, l_ref, acc_ref, first):
    m_new = jnp.max(s_parts[0], axis=1, keepdims=True)
    for s in s_parts[1:]:
        m_new = jnp.maximum(m_new, jnp.max(s, axis=1, keepdims=True))
    if not first:
        m_new = jnp.maximum(m_ref[...], m_new)
    l_new, pv = None, None
    for s, v in zip(s_parts, v_parts):
        p = jnp.exp2(s - m_new)
        l_part = jnp.sum(p, axis=1, keepdims=True)
        pv_part = jnp.dot(p.astype(BF16), v, preferred_element_type=F32)
        l_new = l_part if l_new is None else l_new + l_part
        pv = pv_part if pv is None else pv + pv_part
    if first:
        l_ref[...] = l_new
        acc_ref[...] = pv
    else:
        corr = jnp.exp2(m_ref[...] - m_new)
        l_ref[...] = l_ref[...] * corr + l_new
        acc_ref[...] = acc_ref[...] * corr + pv
    m_ref[...] = m_new


def _scores(q_rows, k_ref):
    s = lax.dot_general(q_rows, k_ref[...].astype(BF16), NT_DIMS, preferred_element_type=F32)
    return s * (ATTN_SCALE * LOG2E)


def _lf_scan_kernel(lf_ref, suf_ref, tot_ref, *, heads):
    lf = lf_ref[...]
    w = lf.shape[1]
    lane = lax.broadcasted_iota(jnp.int32, lf.shape, 1)
    a = lf
    shift = heads
    while shift < w:
        a = a + jnp.where(lane + shift < w, pltpu.roll(a, w - shift, 1), 0.0)
        shift *= 2
    suf_ref[...] = jnp.where(lane + heads < w, pltpu.roll(a, w - heads, 1), 0.0)
    tot = jnp.where(lane < heads, a, 0.0)
    shift = heads
    while shift < w:
        tot = tot + pltpu.roll(tot, shift, 1)
        shift *= 2
    tot_ref[...] = tot


def _lf_scan(lf2, heads, name):
    n_phys, w = lf2.shape
    tp = _pick_tile(n_phys, 256, 8)
    spec = pl.BlockSpec((tp, w), lambda i: (i, 0))
    return pl.pallas_call(
        functools.partial(_lf_scan_kernel, heads=heads),
        grid=(n_phys // tp,),
        in_specs=[spec],
        out_specs=[spec, spec],
        out_shape=[jax.ShapeDtypeStruct((n_phys, w), F32)] * 2,
        compiler_params=_params("parallel"),
        name=name,
    )(lf2)


def _fox_sample_kernel(pt_ref, q_ref, hm_ref, nm_ref, kn_ref, vn_ref, bn_ref, *refs, pps):
    del pt_ref
    k_refs, v_refs = refs[:pps], refs[pps:2 * pps]
    suf_refs, tot_refs = refs[2 * pps:3 * pps], refs[3 * pps:4 * pps]
    o_ref, m_ref, l_ref, acc_ref, carry_ref = refs[4 * pps:]
    pg = pl.program_id(1)
    q_rows = q_ref[0]

    @pl.when(pg == 0)
    def _():
        s = _scores(q_rows, kn_ref.at[0]) + bn_ref[0] * LOG2E + nm_ref[...]
        _online_update([s], [vn_ref[0].astype(BF16)], m_ref, l_ref, acc_ref, first=True)
        carry_ref[...] = jnp.zeros_like(carry_ref)

    carry = carry_ref[...]
    s_parts, v_parts = [], []
    for w in range(pps):
        s_parts.append(_scores(q_rows, k_refs[w]) + (suf_refs[w][0] + carry) * LOG2E + hm_ref[...])
        v_parts.append(v_refs[w][...].astype(BF16))
        carry = carry + tot_refs[w][0]
    carry_ref[...] = carry
    _online_update(s_parts, v_parts, m_ref, l_ref, acc_ref, first=False)

    @pl.when(pg == pl.num_programs(1) - 1)
    def _():
        o_ref[0] = acc_ref[...] / l_ref[...]


def _fox_sample(page_table, q_rows, head_mask, new_mask, k_new, v_new, bias_new, k_pool, v_pool, suf, tot,
                page0, pps, name):
    n_batch, n_pages = page_table.shape
    rows = q_rows.shape[1]
    w = head_mask.shape[1]
    wn = new_mask.shape[1]
    per_b = lambda b, p, pt: (b, 0, 0)
    const = lambda b, p, pt: (0, 0)

    def page(slot, base):
        return lambda b, p, pt: (base + pt[b * n_pages + n_pages - 1 - (p * pps + slot)], 0)

    def page3(slot):
        return lambda b, p, pt: (pt[b * n_pages + n_pages - 1 - (p * pps + slot)], 0, 0)

    grid_spec = pltpu.PrefetchScalarGridSpec(
        num_scalar_prefetch=1,
        grid=(n_batch, n_pages // pps),
        in_specs=[pl.BlockSpec((1, rows, HEAD_DIM), per_b),
                  pl.BlockSpec((rows, w), const), pl.BlockSpec((rows, wn), const),
                  pl.BlockSpec((1, wn, HEAD_DIM), per_b), pl.BlockSpec((1, wn, HEAD_DIM), per_b),
                  pl.BlockSpec((1, 1, wn), per_b)]
                 + [pl.BlockSpec((w, HEAD_DIM), page(s, page0)) for s in range(pps)] * 2
                 + [pl.BlockSpec((1, 1, w), page3(s)) for s in range(pps)] * 2,
        out_specs=pl.BlockSpec((1, rows, HEAD_DIM), per_b),
        scratch_shapes=[pltpu.VMEM((rows, 1), F32), pltpu.VMEM((rows, 1), F32),
                        pltpu.VMEM((rows, HEAD_DIM), F32), pltpu.VMEM((1, w), F32)],
    )
    return pl.pallas_call(
        functools.partial(_fox_sample_kernel, pps=pps),
        grid_spec=grid_spec,
        out_shape=jax.ShapeDtypeStruct((n_batch, rows, HEAD_DIM), F32),
        compiler_params=_params("parallel", "arbitrary"),
        name=name,
    )(page_table.reshape(-1), q_rows, head_mask, new_mask, k_new, v_new, bias_new,
      *([k_pool] * pps), *([v_pool] * pps), *([suf] * pps), *([tot] * pps))


def _rope_kernel(x_ref, c_ref, s_ref, o_ref):
    for hh in range(x_ref.shape[1] // HEAD_DIM):
        sl = slice(hh * HEAD_DIM, (hh + 1) * HEAD_DIM)
        x = x_ref[:, sl]
        o_ref[:, sl] = x * c_ref[...] + pltpu.roll(x, HEAD_DIM // 2, 1) * s_ref[...]


def _rope(p, cos2, sin2, width, name):
    m = p.shape[0]
    tm = _pick_tile(m, ROW_TILE_CAP, 8)
    tc = _pick_tile(width, 512, HEAD_DIM)
    return pl.pallas_call(
        _rope_kernel,
        grid=(m // tm, width // tc),
        in_specs=[pl.BlockSpec((tm, tc), lambda i, j: (i, j)),
                  pl.BlockSpec((tm, HEAD_DIM), lambda i, j: (i, 0)), pl.BlockSpec((tm, HEAD_DIM), lambda i, j: (i, 0))],
        out_specs=pl.BlockSpec((tm, tc), lambda i, j: (i, j)),
        out_shape=jax.ShapeDtypeStruct((m, width), F32),
        compiler_params=_params("parallel", "arbitrary"),
        name=name,
    )(p, cos2, sin2)


def _dot_f32_nt(a, b):
    a3, b3 = _split3(a), _split3(b)
    acc = None
    for ia, ap in enumerate(a3):
        for ib, bp in enumerate(b3):
            if ia + ib > 2:
                continue
            term = lax.dot_general(ap, bp, NT_DIMS, preferred_element_type=F32)
            acc = term if acc is None else acc + term
    return acc


def _topk_mask(gate, n_blocks, topk):
    lane = lax.broadcasted_iota(jnp.int32, gate.shape, 1)
    cnt = jnp.zeros_like(gate)
    for mth in range(n_blocks):
        gm = gate[:, mth:mth + 1]
        ahead = (gm > gate) | ((gm == gate) & (mth < lane))
        cnt = cnt + jnp.where(ahead, 1.0, 0.0)
    return jnp.where((cnt < topk) & (gate > -jnp.inf), 1.0, 0.0)


def _moba_prompt_kernel(q_ref, k_ref, v_ref, o_ref, km_ref, *, blk):
    qi = pl.program_id(2)
    nb = k_ref.shape[0] // blk
    group = q_ref.shape[1] // HEAD_DIM

    @pl.when(qi == 0)
    def _():
        km_ref[...] = jnp.zeros_like(km_ref)
        for n in range(nb):
            km_ref[n:n + 1, :] = jnp.mean(k_ref[n * blk:(n + 1) * blk, :], axis=0, keepdims=True)

    heads = [q_ref[:, r * HEAD_DIM:(r + 1) * HEAD_DIM] for r in range(group)]
    q_sum = heads[0]
    for r in range(1, group):
        q_sum = q_sum + heads[r]
    gate = _dot_f32_nt(q_sum, km_ref[...])
    lane = lax.broadcasted_iota(jnp.int32, gate.shape, 1)
    gate = jnp.where(lane < qi, gate, -jnp.inf)
    drop = (1.0 - _topk_mask(gate, nb - 1, MOBA_TOPK)) * NEG

    for nk in range(1, nb + 1):
        @pl.when(qi == nk - 1)
        def _(nk=nk):
            w = nk * blk
            kb = k_ref[0:w, :].astype(BF16)
            vb = v_ref[0:w, :].astype(BF16)
            bias = [jnp.broadcast_to(drop[:, n:n + 1], (blk, blk)) for n in range(nk - 1)]
            for r in range(group):
                s = lax.dot_general(heads[r].astype(BF16), kb, NT_DIMS, preferred_element_type=F32)
                s = s * (ATTN_SCALE * LOG2E)
                s = _causal_tail(s, blk)
                if nk > 1:
                    s = jnp.concatenate([s[:, n * blk:(n + 1) * blk] + bias[n] for n in range(nk - 1)]
                                        + [s[:, w - blk:]], axis=1)
                o_ref[:, r * HEAD_DIM:(r + 1) * HEAD_DIM] = _softmax_pv(s, vb).astype(o_ref.dtype)


def _moba_prompt(qk, p, n_batch, t, kv_heads, col_v, name):
    blk = MOBA_BLOCK
    gw = MOBA_GROUP * HEAD_DIM
    ck = kv_heads * MOBA_GROUP
    cv = col_v // HEAD_DIM
    assert t // blk <= LANES
    return pl.pallas_call(
        functools.partial(_moba_prompt_kernel, blk=blk),
        grid=(n_batch, kv_heads, t // blk),
        in_specs=[pl.BlockSpec((blk, gw), lambda b, g, i: (b * (t // blk) + i, g)),
                  pl.BlockSpec((t, HEAD_DIM), lambda b, g, i: (b, ck + g)),
                  pl.BlockSpec((t, HEAD_DIM), lambda b, g, i: (b, cv + g))],
        out_specs=pl.BlockSpec((blk, gw), lambda b, g, i: (b * (t // blk) + i, g)),
        out_shape=jax.ShapeDtypeStruct((n_batch * t, kv_heads * gw), BF16),
        scratch_shapes=[pltpu.VMEM((LANES, HEAD_DIM), F32)],
        compiler_params=_params("parallel", "parallel", "arbitrary"),
        name=name,
    )(qk, qk, p)


def _kmean_kernel(pt_ref, *refs, n_in, pages_per_block, kv_heads):
    del pt_ref
    o_ref = refs[n_in]
    blocks = n_in // pages_per_block
    for j in range(blocks):
        tot = None
        for w in range(pages_per_block):
            x_ref = refs[j * pages_per_block + w]
            for tok in range(PAGE_SIZE):
                part = x_ref[tok * kv_heads:(tok + 1) * kv_heads, :]
                tot = part if tot is None else tot + part
        o_ref[0, pl.program_id(1) * blocks + j] = tot / (pages_per_block * PAGE_SIZE)


def _moba_kmean(page_table, k_pool, page0, kv_heads, name):
    n_batch, n_pages = page_table.shape
    ppb = MOBA_BLOCK // PAGE_SIZE
    nb = n_pages // ppb
    blocks = _pick_tile(nb, 4, 1)
    n_in = blocks * ppb
    w = PAGE_SIZE * kv_heads

    def page(slot):
        return lambda b, n, pt: (page0 + pt[b * n_pages + n * n_in + slot], 0)

    grid_spec = pltpu.PrefetchScalarGridSpec(
        num_scalar_prefetch=1,
        grid=(n_batch, nb // blocks),
        in_specs=[pl.BlockSpec((w, HEAD_DIM), page(s)) for s in range(n_in)],
        out_specs=pl.BlockSpec((1, nb, kv_heads, HEAD_DIM), lambda b, n, pt: (b, 0, 0, 0)),
    )
    return pl.pallas_call(
        functools.partial(_kmean_kernel, n_in=n_in, pages_per_block=ppb, kv_heads=kv_heads),
        grid_spec=grid_spec,
        out_shape=jax.ShapeDtypeStruct((n_batch, nb, kv_heads, HEAD_DIM), F32),
        compiler_params=_params("parallel", "arbitrary"),
        name=name,
    )(page_table.reshape(-1), *([k_pool] * n_in))


def _moba_sample_kernel(pt_ref, q_ref, qr_ref, hm_ref, nm_ref, km_ref, kn_ref, vn_ref, *refs,
                        pps, kv_heads, n_new):
    del pt_ref
    k_refs, v_refs = refs[:pps], refs[pps:2 * pps]
    o_ref, m_ref, l_ref, acc_ref, sel_ref = refs[2 * pps:]
    pg = pl.program_id(1)
    q_rows = qr_ref[0]
    rows_per_head = MOBA_GROUP * n_new
    nb = km_ref.shape[2]
    ppb = MOBA_BLOCK // PAGE_SIZE

    @pl.when(pg == 0)
    def _():
        q = q_ref[0]
        for g in range(kv_heads):
            q_sum = q[:, g * MOBA_GROUP * HEAD_DIM:(g * MOBA_GROUP + 1) * HEAD_DIM]
            for r in range(1, MOBA_GROUP):
                c0 = (g * MOBA_GROUP + r) * HEAD_DIM
                q_sum = q_sum + q[:, c0:c0 + HEAD_DIM]
            sel = _topk_mask(_dot_f32_nt(q_sum, km_ref[0, g]), nb, MOBA_TOPK)
            for r in range(MOBA_GROUP):
                r0 = g * rows_per_head + r * n_new
                sel_ref[r0:r0 + n_new, 0:nb] = sel
        s = _scores(q_rows, kn_ref.at[0]) + nm_ref[...]
        _online_update([s], [vn_ref[0].astype(BF16)], m_ref, l_ref, acc_ref, first=True)

    lane = lax.broadcasted_iota(jnp.int32, (q_rows.shape[0], nb), 1)
    s_parts, v_parts = [], []
    for w in range(pps):
        if w % ppb == 0:
            blk = (pg * pps + w) // ppb
            keep = jnp.sum(jnp.where(lane == blk, sel_ref[:, 0:nb], 0.0), axis=1, keepdims=True)
            drop = (1.0 - keep) * NEG
        s_parts.append(_scores(q_rows, k_refs[w]) + drop + hm_ref[...])
        v_parts.append(v_refs[w][...].astype(BF16))
    _online_update(s_parts, v_parts, m_ref, l_ref, acc_ref, first=False)

    @pl.when(pg == pl.num_programs(1) - 1)
    def _():
        o_ref[0] = acc_ref[...] / l_ref[...]


def _moba_sample(page_table, q, q_rows, head_mask, new_mask, k_mean, k_new, v_new, k_pool, v_pool, page0, pps,
                 kv_heads, n_new, name):
    n_batch, n_pages = page_table.shape
    rows = q_rows.shape[1]
    w = head_mask.shape[1]
    wn = new_mask.shape[1]
    nb = k_mean.shape[2]
    assert nb <= LANES and pps % (MOBA_BLOCK // PAGE_SIZE) == 0
    per_b = lambda b, p, pt: (b, 0, 0)
    const = lambda b, p, pt: (0, 0)

    def page(slot):
        return lambda b, p, pt: (page0 + pt[b * n_pages + p * pps + slot], 0)

    grid_spec = pltpu.PrefetchScalarGridSpec(
        num_scalar_prefetch=1,
        grid=(n_batch, n_pages // pps),
        in_specs=[pl.BlockSpec((1, n_new, q.shape[2]), per_b), pl.BlockSpec((1, rows, HEAD_DIM), per_b),
                  pl.BlockSpec((rows, w), const), pl.BlockSpec((rows, wn), const),
                  pl.BlockSpec((1, kv_heads, nb, HEAD_DIM), lambda b, p, pt: (b, 0, 0, 0)),
                  pl.BlockSpec((1, wn, HEAD_DIM), per_b), pl.BlockSpec((1, wn, HEAD_DIM), per_b)]
                 + [pl.BlockSpec((w, HEAD_DIM), page(s)) for s in range(pps)] * 2,
        out_specs=pl.BlockSpec((1, rows, HEAD_DIM), per_b),
        scratch_shapes=[pltpu.VMEM((rows, 1), F32), pltpu.VMEM((rows, 1), F32),
                        pltpu.VMEM((rows, HEAD_DIM), F32), pltpu.VMEM((rows, LANES), F32)],
    )
    return pl.pallas_call(
        functools.partial(_moba_sample_kernel, pps=pps, kv_heads=kv_heads, n_new=n_new),
        grid_spec=grid_spec,
        out_shape=jax.ShapeDtypeStruct((n_batch, rows, HEAD_DIM), F32),
        compiler_params=_params("parallel", "arbitrary"),
        name=name,
    )(page_table.reshape(-1), q, q_rows, head_mask, new_mask, k_mean, k_new, v_new,
      *([k_pool] * pps), *([v_pool] * pps))


def _cross_kernel(q_ref, k_ref, v_ref, o_ref):
    s = lax.dot_general(q_ref[...].astype(BF16), k_ref[...].astype(BF16), NT_DIMS, preferred_element_type=F32)
    s = s * (CROSS_SCALE * LOG2E)
    m = jnp.max(s, axis=1, keepdims=True)
    p = jnp.exp2(s - m)
    l = jnp.sum(p, axis=1, keepdims=True)
    o = jnp.dot(p.astype(BF16), v_ref[...].astype(BF16), preferred_element_type=F32)
    o_ref[...] = (o / l).astype(o_ref.dtype)


def _cross(q, mem_k, mem_v, row0, n_batch, t, out_dtype, name):
    mem = mem_k.shape[0] // n_batch
    tq = _pick_tile(t, 512, 8)
    nq = t // tq
    r0 = row0 // tq
    return pl.pallas_call(
        _cross_kernel,
        grid=(n_batch, nq, CROSS_HEADS),
        in_specs=[pl.BlockSpec((tq, CROSS_HEAD_DIM), lambda b, i, h: (r0 + b * nq + i, h)),
                  pl.BlockSpec((mem, CROSS_HEAD_DIM), lambda b, i, h: (b, h)),
                  pl.BlockSpec((mem, CROSS_HEAD_DIM), lambda b, i, h: (b, h))],
        out_specs=pl.BlockSpec((tq, CROSS_HEAD_DIM), lambda b, i, h: (b * nq + i, h)),
        out_shape=jax.ShapeDtypeStruct((n_batch * t, CROSS_HEADS * CROSS_HEAD_DIM), out_dtype),
        compiler_params=_params("parallel", "parallel", "parallel"),
        name=name,
    )(q, mem_k, mem_v)


def _head_rows(q):
    nb, t, w = q.shape
    return q.reshape(nb, t, w // HEAD_DIM, HEAD_DIM).transpose(0, 2, 1, 3).reshape(nb, -1, HEAD_DIM).astype(BF16)


def _decode_masks(q_heads, kv_heads, n_new):
    row = jnp.arange(q_heads * n_new)
    own = ((row // n_new) // (q_heads // kv_heads))[:, None]
    t_q = (row % n_new)[:, None]
    lane = jnp.arange(PAGE_SIZE * kv_heads)[None, :]
    head_mask = jnp.where(lane % kv_heads == own, 0.0, NEG).astype(F32)
    lane = jnp.arange(n_new * kv_heads)[None, :]
    new_mask = jnp.where((lane % kv_heads == own) & (lane // kv_heads <= t_q), 0.0, NEG).astype(F32)
    return head_mask, new_mask


def kernel(x_prompt, x_sample, cache_fox_k, cache_fox_v, cache_fox_logf, state_conv, cache_moba_k, cache_moba_v, cache_mem_k, cache_mem_v, page_table, mem_prompt, w_in_even, b_forget, w_conv, w_out_even, w_in_odd, w_out_odd, w_cross_q, w_cross_k, w_cross_v, w_cross_o, w_ffa_gate, w_ffa_up, w_ffa_down, w_ffb_gate, w_ffb_up, w_ffb_down, ln_g, ln_b):
    bp, t, d = x_prompt.shape
    bs, ts, _ = x_sample.shape
    depth = ln_g.shape[0]
    mp, ms = bp * t, bs * ts
    m = mp + ms
    alpha = (2 * depth) ** 0.25
    conv_ch = d // 2
    fox_heads = conv_ch // HEAD_DIM
    fox_w = fox_heads * HEAD_DIM
    moba_heads = d // HEAD_DIM
    kv_heads = moba_heads // MOBA_GROUP
    kv_w = kv_heads * HEAD_DIM
    cross_w = CROSS_HEADS * CROSS_HEAD_DIM
    mem_len = mem_prompt.shape[1]
    n_pages = page_table.shape[1]
    past_len = n_pages * PAGE_SIZE
    new_pad = BF16_SUBLANES

    wb = {"in_even": [w_in_even[l:l + 1].astype(BF16) for l in range(w_in_even.shape[0])]}
    wb |= {name: _to_bf16(w, "cast_" + name) for name, w in (
        ("out_even", w_out_even), ("in_odd", w_in_odd), ("out_odd", w_out_odd),
        ("cross_q", w_cross_q), ("cross_k", w_cross_k), ("cross_v", w_cross_v), ("cross_o", w_cross_o),
        ("ffa_gate", w_ffa_gate), ("ffa_up", w_ffa_up), ("ffa_down", w_ffa_down),
        ("ffb_gate", w_ffb_gate), ("ffb_up", w_ffb_up), ("ffb_down", w_ffb_down))}
    def ffn(act, which, layer, g, b, name):
        hidden = _swiglu(act, wb[which + "_gate"], wb[which + "_up"], layer, name + "_swiglu")
        return _resid(hidden, wb[which + "_down"], layer, act, g, b, alpha, 0.5, name + "_down", tn_cap=256)

    pos = jnp.concatenate([jnp.tile(jnp.arange(t, dtype=jnp.int32), bp),
                           jnp.tile(past_len + jnp.arange(ts, dtype=jnp.int32), bs)])
    half = HEAD_DIM // 2
    inv_freq = ROPE_THETA ** (-jnp.arange(half, dtype=F32) / half)
    ang = pos.astype(F32)[:, None] * inv_freq[None, :]
    cos2 = jnp.concatenate([jnp.cos(ang), jnp.cos(ang)], axis=1)
    sin2 = jnp.concatenate([-jnp.sin(ang), jnp.sin(ang)], axis=1)

    ones = jnp.ones((1, d), F32)
    zeros = jnp.zeros((1, d), F32)
    unit_stats = lambda rows: jnp.zeros((rows, LANES), F32).at[:, 1].set(1.0)

    mem_act = (mem_prompt.reshape(bp * mem_len, d), unit_stats(bp * mem_len), ones, zeros)
    mem_k_p = jnp.stack([_proj(mem_act, wb["cross_k"], i, cross_w, F32, f"mem_k{i}") for i in range(depth)])
    mem_v_p = jnp.stack([_proj(mem_act, wb["cross_v"], i, cross_w, F32, f"mem_v{i}") for i in range(depth)])

    x0 = jnp.concatenate([x_prompt.reshape(mp, d), x_sample.reshape(ms, d)], axis=0)
    act = (x0, unit_stats(m), ones, zeros)
    st_p = {name: [] for name in ("fox_k", "fox_v", "fox_logf", "conv", "moba_k", "moba_v")}
    st_s = {name: [] for name in st_p}

    for i in range(depth):
        g = [ln_g[i, s].reshape(1, d) for s in range(4)]
        b = [ln_b[i, s].reshape(1, d) for s in range(4)]
        j = i // 2
        act = ffn(act, "ffa", i, g[0], b[0], f"ffa{i}")

        if i % 2 == 0:
            n_main = 3 * conv_ch + 3 * fox_w
            w_f = jnp.pad(w_in_even[j, :, n_main:].astype(BF16), ((0, 0), (0, LANES - fox_heads)))[None]
            p, f = _proj(act, wb["in_even"], j, n_main, F32, f"in_even{i}", w_aux=w_f)
            b_f = jnp.pad(b_forget[j].astype(F32), (0, LANES - fox_heads)).reshape(1, LANES)
            col_q, col_k, col_v = 3 * conv_ch, 3 * conv_ch + fox_w, 3 * conv_ch + 2 * fox_w

            a_p, conv_p = _conv(p, jnp.zeros((bp, CONV_WIDTH - 1, conv_ch), F32), w_conv[j], 0, bp, t, conv_ch,
                                BF16, f"conv_p{i}")
            a_s, conv_s = _conv(p, state_conv[j], w_conv[j], mp, bs, ts, conv_ch, F32, f"conv_s{i}")
            a_s = a_s.astype(BF16)

            lf_p, nc_p = _logf(f, b_f, 0, bp, t, f"logf_p{i}")
            lf_s, nc_s = _logf(f, b_f, mp, bs, ts, f"logf_s{i}")
            tq = _pick_tile(t, 256, LANES)
            nct = nc_p[:, :fox_heads].reshape(bp, t, fox_heads).transpose(0, 2, 1).reshape(bp, fox_heads, 1, t)
            o_p = _fox_prompt(p, nct, bp, t, fox_heads, col_q, tq, f"fox_p{i}")

            ps = p[mp:]
            q_s = ps[:, col_q:col_q + fox_w].reshape(bs, ts, fox_w)
            k_s = ps[:, col_k:col_k + fox_w].reshape(bs, ts, fox_w)
            v_s = ps[:, col_v:col_v + fox_w].reshape(bs, ts, fox_w)
            head_mask, new_mask = _decode_masks(fox_heads, fox_heads, ts)
            bias_new = nc_s[:, :fox_heads].reshape(bs, 1, ts * fox_heads)
            n_phys = cache_fox_k.shape[1]
            lf_suffix, lf_total = _lf_scan(cache_fox_logf[j].reshape(n_phys, PAGE_SIZE * fox_heads), fox_heads,
                                           f"lf_scan{i}")
            o_s = _fox_sample(page_table, _head_rows(q_s), head_mask, new_mask,
                              k_s.reshape(bs, ts * fox_heads, HEAD_DIM), v_s.reshape(bs, ts * fox_heads, HEAD_DIM),
                              bias_new, cache_fox_k.reshape(-1, HEAD_DIM), cache_fox_v.reshape(-1, HEAD_DIM),
                              lf_suffix.reshape(n_phys, 1, -1), lf_total.reshape(n_phys, 1, -1),
                              j * n_phys, FOX_PAGES_PER_STEP, f"fox_s{i}")
            o_s = o_s.reshape(bs, fox_heads, ts, HEAD_DIM).transpose(0, 2, 1, 3).reshape(ms, fox_w).astype(BF16)

            mix_in = jnp.concatenate([jnp.concatenate([a_p, o_p], axis=1), jnp.concatenate([a_s, o_s], axis=1)], axis=0)
            w_out = wb["out_even"]

            st_p["fox_k"].append(p[:mp, col_k:col_k + fox_w].reshape(bp, t, fox_heads, HEAD_DIM))
            st_p["fox_v"].append(p[:mp, col_v:col_v + fox_w].reshape(bp, t, fox_heads, HEAD_DIM))
            st_p["fox_logf"].append(lf_p[:, :fox_heads].reshape(bp, t, fox_heads))
            st_p["conv"].append(conv_p)
            st_s["fox_k"].append(k_s.reshape(bs, ts, fox_heads, HEAD_DIM))
            st_s["fox_v"].append(v_s.reshape(bs, ts, fox_heads, HEAD_DIM))
            st_s["fox_logf"].append(lf_s[:, :fox_heads].reshape(bs, ts, fox_heads))
            st_s["conv"].append(conv_s)
        else:
            p = _proj(act, wb["in_odd"], j, d + 2 * kv_w, F32, f"in_odd{i}")
            qk_w = d + kv_w
            qk = _rope(p, cos2, sin2, qk_w, f"rope{i}")
            o_p = _moba_prompt(qk, p, bp, t, kv_heads, qk_w, f"moba_p{i}")

            q_s = qk[mp:, :d].reshape(bs, ts, d)
            k_s = qk[mp:, d:].reshape(bs, ts, kv_w)
            v_s = p[mp:, qk_w:].reshape(bs, ts, kv_w)
            head_mask, new_mask = _decode_masks(moba_heads, kv_heads, ts)
            n_phys = cache_moba_k.shape[1]
            k_pool = cache_moba_k.reshape(-1, HEAD_DIM)
            v_pool = cache_moba_v.reshape(-1, HEAD_DIM)
            k_mean = _moba_kmean(page_table, k_pool, j * n_phys, kv_heads, f"moba_kmean{i}").transpose(0, 2, 1, 3)
            o_s = _moba_sample(page_table, q_s, _head_rows(q_s), head_mask, new_mask, k_mean,
                               k_s.reshape(bs, ts * kv_heads, HEAD_DIM), v_s.reshape(bs, ts * kv_heads, HEAD_DIM),
                               k_pool, v_pool, j * n_phys, MOBA_PAGES_PER_STEP, kv_heads, ts, f"moba_s{i}")
            o_s = o_s.reshape(bs, moba_heads, ts, HEAD_DIM).transpose(0, 2, 1, 3).reshape(ms, d).astype(BF16)

            mix_in = jnp.concatenate([o_p, o_s], axis=0)
            w_out = wb["out_odd"]

            st_p["moba_k"].append(qk[:mp, d:].reshape(bp, t, kv_heads, HEAD_DIM))
            st_p["moba_v"].append(p[:mp, qk_w:].reshape(bp, t, kv_heads, HEAD_DIM))
            st_s["moba_k"].append(k_s.reshape(bs, ts, kv_heads, HEAD_DIM))
            st_s["moba_v"].append(v_s.reshape(bs, ts, kv_heads, HEAD_DIM))

        act = _resid(mix_in, w_out, j, act, g[1], b[1], alpha, 1.0, f"mix_out{i}")

        q = _proj(act, wb["cross_q"], i, cross_w, F32, f"cross_q{i}")
        o_p = _cross(q, mem_k_p[i], mem_v_p[i], 0, bp, t, BF16, f"cross_p{i}")
        o_s = _cross(q, cache_mem_k[i].reshape(bs * mem_len, cross_w), cache_mem_v[i].reshape(bs * mem_len, cross_w),
                     mp, bs, ts, F32, f"cross_s{i}").astype(BF16)
        act = _resid(jnp.concatenate([o_p, o_s], axis=0), wb["cross_o"], i, act, g[2], b[2], alpha, 1.0,
                     f"cross_o{i}")

        act = ffn(act, "ffb", i, g[3], b[3], f"ffb{i}")

    y_prompt = _final_norm(act, 0, mp, "y_prompt").reshape(bp, t, d)
    y_sample = _final_norm(act, mp, ms, "y_sample").reshape(bs, ts, d)
    stack = lambda xs: jnp.stack(xs)
    mem_shape = (depth, bp, mem_len, CROSS_HEADS, CROSS_HEAD_DIM)
    return (y_prompt, y_sample,
            stack(st_p["fox_k"]), stack(st_p["fox_v"]), stack(st_p["fox_logf"]), stack(st_p["conv"]),
            stack(st_p["moba_k"]), stack(st_p["moba_v"]),
            mem_k_p.reshape(mem_shape), mem_v_p.reshape(mem_shape),
            stack(st_s["fox_k"]), stack(st_s["fox_v"]), stack(st_s["fox_logf"]), stack(st_s["conv"]),
            stack(st_s["moba_k"]), stack(st_s["moba_v"]))
```

```python
import functools
import math

import jax
import jax.numpy as jnp
from jax import lax
from jax.experimental import pallas as pl
from jax.experimental.pallas import tpu as pltpu

F32 = jnp.float32
BF16 = jnp.bfloat16

HEAD_DIM = 128
CONV_WIDTH = 3
PAGE_SIZE = 128
MOBA_BLOCK = 256
MOBA_TOPK = 3
MOBA_GROUP = 4
CROSS_HEADS = 4
CROSS_HEAD_DIM = 256
ROPE_THETA = 10000.0
LN_EPS = 1e-5
ATTN_SCALE = HEAD_DIM ** -0.5
CROSS_SCALE = CROSS_HEAD_DIM ** -0.5

LANES = 128
BF16_SUBLANES = 16
VMEM_LIMIT_BYTES = 56 * 1024 * 1024
ROW_TILE_CAP = 704
COL_TILE_CAP = 512
PROJ_COL_TILE_CAP = 512
SWIGLU_TILE_COLS = 256
SWIGLU_TILES = 2
LOG2E = 1.4426950408889634
CAST_BLOCK_BYTES = 16 * 1024 * 1024
FOX_HEADS_PER_STEP = 4
FOX_PAGES_PER_STEP = 4
MOBA_PAGES_PER_STEP = 8
NEG = -1e30
NT_DIMS = (((1,), (1,)), ((), ()))


def _pick_tile(n, cap, mult):
    best = None
    for d in range(mult, min(n, cap) + 1, mult):
        if n % d == 0:
            best = d
    if best is None:
        raise ValueError(f"no tile for {n} (cap {cap}, multiple of {mult})")
    return best


def _params(*sem):
    return pltpu.CompilerParams(dimension_semantics=sem, vmem_limit_bytes=VMEM_LIMIT_BYTES)


def _normalize(z, st, g, b):
    return (z - st[:, 0:1]) * st[:, 1:2] * g + b


def _split3(x):
    hi = x.astype(BF16)
    r = x - hi.astype(F32)
    mid = r.astype(BF16)
    lo = (r - mid.astype(F32)).astype(BF16)
    return hi, mid, lo


def _dot01(a01, x, dims=(((1,), (0,)), ((), ()))):
    hi, mid, lo = _split3(x)
    acc = lax.dot_general(a01, hi, dims, preferred_element_type=F32)
    acc = acc + lax.dot_general(a01, mid, dims, preferred_element_type=F32)
    return acc + lax.dot_general(a01, lo, dims, preferred_element_type=F32)


def _proj_kernel(z_ref, st_ref, g_ref, b_ref, *rest, swiglu_tiles, has_aux):
    n_main = 2 * swiglu_tiles if swiglu_tiles else 1
    n_in = n_main + has_aux
    w_refs, o_ref, xb_ref = rest[:n_main], rest[n_in], rest[-1]
    if has_aux:
        w_aux_ref, o_aux_ref = rest[n_main], rest[n_in + 1]

    @pl.when(pl.program_id(1) == 0)
    def _():
        xb_ref[...] = _normalize(z_ref[...], st_ref[...], g_ref[...], b_ref[...]).astype(BF16)
        if has_aux:
            o_aux_ref[...] = jnp.dot(xb_ref[...], w_aux_ref[...], preferred_element_type=F32)

    xb = xb_ref[...]
    if swiglu_tiles:
        tn = w_refs[0].shape[1]
        for h in range(swiglu_tiles):
            a = jnp.dot(xb, w_refs[h][...], preferred_element_type=F32)
            c = jnp.dot(xb, w_refs[swiglu_tiles + h][...], preferred_element_type=F32)
            o_ref[:, h * tn:(h + 1) * tn] = (a * (1.0 / (1.0 + jnp.exp(-a))) * c).astype(o_ref.dtype)
    else:
        o_ref[...] = jnp.dot(xb, w_refs[0][...], preferred_element_type=F32).astype(o_ref.dtype)


def _proj(act, w, layer, n, out_dtype, name, w_aux=None):
    w = w[layer]
    z, st, g, b = act
    m, k = z.shape
    tm = _pick_tile(m, ROW_TILE_CAP, BF16_SUBLANES)
    tn = _pick_tile(n, PROJ_COL_TILE_CAP, LANES)
    row = lambda i, j: (i, 0)
    has_aux = w_aux is not None
    in_specs = [pl.BlockSpec((tm, k), row), pl.BlockSpec((tm, LANES), row),
                pl.BlockSpec((1, k), lambda i, j: (0, 0)), pl.BlockSpec((1, k), lambda i, j: (0, 0)),
                pl.BlockSpec((None, k, tn), lambda i, j: (0, 0, j))]
    out_specs = [pl.BlockSpec((tm, tn), lambda i, j: (i, j))]
    out_shape = [jax.ShapeDtypeStruct((m, n), out_dtype)]
    operands = [z, st, g, b, w]
    if has_aux:
        in_specs.append(pl.BlockSpec((None, k, LANES), lambda i, j: (0, 0, 0)))
        out_specs.append(pl.BlockSpec((tm, LANES), row))
        out_shape.append(jax.ShapeDtypeStruct((m, LANES), F32))
        operands.append(w_aux)
    out = pl.pallas_call(
        functools.partial(_proj_kernel, swiglu_tiles=0, has_aux=has_aux),
        grid=(m // tm, n // tn),
        in_specs=in_specs,
        out_specs=out_specs,
        out_shape=out_shape,
        scratch_shapes=[pltpu.VMEM((tm, k), BF16)],
        compiler_params=_params("parallel", "arbitrary"),
        name=name,
    )(*operands)
    return out if has_aux else out[0]


def _swiglu(act, w_gate, w_up, layer, name):
    w_gate, w_up = w_gate[layer], w_up[layer]
    z, st, g, b = act
    m, k = z.shape
    n = w_gate.shape[2]
    tm = _pick_tile(m, ROW_TILE_CAP, BF16_SUBLANES)
    tn = _pick_tile(n, SWIGLU_TILE_COLS, LANES)
    tiles = n // tn
    steps = -(-tiles // SWIGLU_TILES)
    row = lambda i, j: (i, 0)

    def tile(h):
        return lambda i, j: (0, 0, jnp.minimum(j * SWIGLU_TILES + h, tiles - 1))

    w_specs = [pl.BlockSpec((None, k, tn), tile(h)) for h in range(SWIGLU_TILES)]
    return pl.pallas_call(
        functools.partial(_proj_kernel, swiglu_tiles=SWIGLU_TILES, has_aux=False),
        grid=(m // tm, steps),
        in_specs=[pl.BlockSpec((tm, k), row), pl.BlockSpec((tm, LANES), row),
                  pl.BlockSpec((1, k), lambda i, j: (0, 0)), pl.BlockSpec((1, k), lambda i, j: (0, 0))]
                 + w_specs + w_specs,
        out_specs=pl.BlockSpec((tm, SWIGLU_TILES * tn), lambda i, j: (i, j)),
        out_shape=jax.ShapeDtypeStruct((m, steps * SWIGLU_TILES * tn), BF16),
        scratch_shapes=[pltpu.VMEM((tm, k), BF16)],
        compiler_params=_params("parallel", "arbitrary"),
        name=name,
    )(z, st, g, b, *([w_gate] * SWIGLU_TILES), *([w_up] * SWIGLU_TILES))


def _resid_kernel(a_ref, w_ref, zr_ref, sr_ref, gr_ref, br_ref, zo_ref, so_ref, mean_ref, m2_ref,
                  *, alpha, scale, n_total):
    j = pl.program_id(1)
    nj = n_total // zo_ref.shape[1]
    y = jnp.dot(a_ref[...], w_ref[...], preferred_element_type=F32)
    x_res = _normalize(zr_ref[...], sr_ref[...], gr_ref[...], br_ref[...])
    z_new = alpha * x_res + scale * y
    zo_ref[...] = z_new
    tn = z_new.shape[1]
    mu_t = jnp.mean(z_new, axis=1, keepdims=True)
    d = z_new - mu_t
    m2_t = jnp.sum(d * d, axis=1, keepdims=True)
    lane = lax.broadcasted_iota(jnp.int32, mean_ref.shape, 1)

    @pl.when(j == 0)
    def _():
        mean_ref[...] = jnp.zeros_like(mean_ref)
        m2_ref[...] = jnp.zeros_like(m2_ref)

    mean_ref[...] = jnp.where(lane == j, mu_t, mean_ref[...])
    m2_ref[...] = jnp.where(lane == j, m2_t, m2_ref[...])

    @pl.when(j == nj - 1)
    def _():
        mus = mean_ref[...]
        mean = jnp.sum(mus, axis=1, keepdims=True) / nj
        dev = jnp.where(lane < nj, mus - mean, 0.0)
        m2 = jnp.sum(m2_ref[...], axis=1, keepdims=True) + tn * jnp.sum(dev * dev, axis=1, keepdims=True)
        rstd = lax.rsqrt(m2 / n_total + LN_EPS)
        so_ref[...] = jnp.where(lane == 0, mean, jnp.where(lane == 1, rstd, 0.0))


def _resid(a, w, layer, act, ln_g, ln_b, alpha, scale, name, tn_cap=COL_TILE_CAP):
    w = w[layer]
    z, st, g, b = act
    m = a.shape[0]
    k, n = w.shape[1:]
    tm = _pick_tile(m, ROW_TILE_CAP, BF16_SUBLANES)
    tn = _pick_tile(n, tn_cap, LANES)
    assert n // tn <= LANES
    z_new, st_new = pl.pallas_call(
        functools.partial(_resid_kernel, alpha=alpha, scale=scale, n_total=n),
        grid=(m // tm, n // tn),
        in_specs=[pl.BlockSpec((tm, k), lambda i, j: (i, 0)), pl.BlockSpec((None, k, tn), lambda i, j: (0, 0, j)),
                  pl.BlockSpec((tm, tn), lambda i, j: (i, j)), pl.BlockSpec((tm, LANES), lambda i, j: (i, 0)),
                  pl.BlockSpec((1, tn), lambda i, j: (0, j)), pl.BlockSpec((1, tn), lambda i, j: (0, j))],
        out_specs=[pl.BlockSpec((tm, tn), lambda i, j: (i, j)), pl.BlockSpec((tm, LANES), lambda i, j: (i, 0))],
        out_shape=[jax.ShapeDtypeStruct((m, n), F32), jax.ShapeDtypeStruct((m, LANES), F32)],
        scratch_shapes=[pltpu.VMEM((tm, LANES), F32), pltpu.VMEM((tm, LANES), F32)],
        compiler_params=_params("parallel", "arbitrary"),
        name=name,
    )(a, w, z, st, g, b)
    return z_new, st_new, ln_g, ln_b


def _cast_kernel(*refs):
    n = len(refs) // 2
    for x_ref, o_ref in zip(refs[:n], refs[n:]):
        o_ref[...] = x_ref[...].astype(o_ref.dtype)


def _to_bf16(w, name):
    l, k, n = w.shape
    tr = _pick_tile(k, max(BF16_SUBLANES, CAST_BLOCK_BYTES // (4 * n * l)), BF16_SUBLANES)

    def layer(i):
        return lambda r: (i, r, 0)

    return pl.pallas_call(
        _cast_kernel,
        grid=(k // tr,),
        in_specs=[pl.BlockSpec((1, tr, n), layer(i)) for i in range(l)],
        out_specs=[pl.BlockSpec((1, tr, n), lambda r: (0, r, 0)) for _ in range(l)],
        out_shape=[jax.ShapeDtypeStruct((1, k, n), BF16) for _ in range(l)],
        compiler_params=_params("parallel"),
        name=name,
    )(*([w] * l))


def _final_norm_kernel(z_ref, st_ref, g_ref, b_ref, o_ref):
    o_ref[...] = _normalize(z_ref[...], st_ref[...], g_ref[...], b_ref[...])


def _final_norm(act, row0, rows, name):
    z, st, g, b = act
    d = z.shape[1]
    tm = _pick_tile(math.gcd(rows, row0) if row0 else rows, 512, 8)
    off = row0 // tm
    return pl.pallas_call(
        _final_norm_kernel,
        grid=(rows // tm,),
        in_specs=[pl.BlockSpec((tm, d), lambda i: (i + off, 0)), pl.BlockSpec((tm, LANES), lambda i: (i + off, 0)),
                  pl.BlockSpec((1, d), lambda i: (0, 0)), pl.BlockSpec((1, d), lambda i: (0, 0))],
        out_specs=pl.BlockSpec((tm, d), lambda i: (i, 0)),
        out_shape=jax.ShapeDtypeStruct((rows, d), F32),
        compiler_params=_params("parallel"),
        name=name,
    )(z, st, g, b)


def _conv_kernel(ain_ref, ab_ref, ac_ref, buf_ref, w_ref, o_ref, nb_ref):
    u = ac_ref[...] * ain_ref[...]
    t = u.shape[0]
    row = lax.broadcasted_iota(jnp.int32, u.shape, 0)
    b0 = buf_ref[0, 0:1, :]
    b1 = buf_ref[0, 1:2, :]
    u1 = jnp.where(row == 0, b1, pltpu.roll(u, 1, 0))
    u2 = jnp.where(row == 0, b0, jnp.where(row == 1, b1, pltpu.roll(u, 2, 0)))
    y = w_ref[0:1, :] * u2 + w_ref[1:2, :] * u1 + w_ref[2:3, :] * u
    o_ref[...] = (ab_ref[...] * y).astype(o_ref.dtype)
    nb_ref[0] = u[t - (CONV_WIDTH - 1):, :]


def _conv(p, buf, w, row0, n_batch, t, ch, out_dtype, name):
    tc = _pick_tile(ch, 512, LANES)
    nc = ch // tc
    r0 = row0 // t
    return pl.pallas_call(
        _conv_kernel,
        grid=(n_batch, nc),
        in_specs=[pl.BlockSpec((t, tc), lambda b, c: (b + r0, c)),
                  pl.BlockSpec((t, tc), lambda b, c: (b + r0, c + nc)),
                  pl.BlockSpec((t, tc), lambda b, c: (b + r0, c + 2 * nc)),
                  pl.BlockSpec((1, CONV_WIDTH - 1, tc), lambda b, c: (b, 0, c)),
                  pl.BlockSpec((CONV_WIDTH, tc), lambda b, c: (0, c))],
        out_specs=[pl.BlockSpec((t, tc), lambda b, c: (b, c)),
                   pl.BlockSpec((1, CONV_WIDTH - 1, tc), lambda b, c: (b, 0, c))],
        out_shape=[jax.ShapeDtypeStruct((n_batch * t, ch), out_dtype),
                   jax.ShapeDtypeStruct((n_batch, CONV_WIDTH - 1, ch), F32)],
        compiler_params=_params("parallel", "parallel"),
        name=name,
    )(p, p, p, buf, w)


def _logf_kernel(f_ref, bf_ref, lf_ref, nc_ref):
    x = f_ref[...] + bf_ref[...]
    lf = jnp.minimum(x, 0.0) - jnp.log1p(jnp.exp(-jnp.abs(x)))
    lf_ref[...] = lf
    row = lax.broadcasted_iota(jnp.int32, lf.shape, 0)
    cs = lf
    shift = 1
    while shift < lf.shape[0]:
        cs = cs + jnp.where(row >= shift, pltpu.roll(cs, shift, 0), 0.0)
        shift *= 2
    nc_ref[...] = -cs


def _logf(f, bias, row0, n_batch, t, name):
    r0 = row0 // t
    return pl.pallas_call(
        _logf_kernel,
        grid=(n_batch,),
        in_specs=[pl.BlockSpec((t, LANES), lambda b: (b + r0, 0)), pl.BlockSpec((1, LANES), lambda b: (0, 0))],
        out_specs=[pl.BlockSpec((t, LANES), lambda b: (b, 0)), pl.BlockSpec((t, LANES), lambda b: (b, 0))],
        out_shape=[jax.ShapeDtypeStruct((n_batch * t, LANES), F32)] * 2,
        compiler_params=_params("parallel"),
        name=name,
    )(f, bias)


def _softmax_pv(s, vb):
    m = jnp.max(s, axis=1, keepdims=True)
    p = jnp.exp2(s - m)
    l = jnp.sum(p, axis=1, keepdims=True)
    return jnp.dot(p.astype(BF16), vb, preferred_element_type=F32) / l


def _causal_tail(s, width):
    row = lax.broadcasted_iota(jnp.int32, (s.shape[0], width), 0)
    col = lax.broadcasted_iota(jnp.int32, (s.shape[0], width), 1)
    tail = jnp.where(col <= row, s[:, s.shape[1] - width:], NEG)
    if s.shape[1] == width:
        return tail
    return jnp.concatenate([s[:, :s.shape[1] - width], tail], axis=1)


def _fox_prompt_kernel(q_ref, k_ref, v_ref, nc_ref, o_ref, *, tq):
    hg = pl.program_id(1)
    qi = pl.program_id(2)
    hp = q_ref.shape[1] // HEAD_DIM
    for nk in range(1, k_ref.shape[0] // tq + 1):
        @pl.when(qi == nk - 1)
        def _(nk=nk):
            w = nk * tq
            for hh in range(hp):
                cols = slice(hh * HEAD_DIM, (hh + 1) * HEAD_DIM)
                qb = q_ref[:, cols].astype(BF16)
                kb = k_ref[0:w, cols].astype(BF16)
                vb = v_ref[0:w, cols].astype(BF16)
                s = (lax.dot_general(qb, kb, NT_DIMS, preferred_element_type=F32) * (ATTN_SCALE * LOG2E)
                     + nc_ref[0, hg * hp + hh, :, 0:w] * LOG2E)
                o_ref[:, cols] = _softmax_pv(_causal_tail(s, tq), vb).astype(o_ref.dtype)


def _fox_prompt(p, nct, n_batch, t, heads, col_q, tq, name):
    hp = FOX_HEADS_PER_STEP if heads % FOX_HEADS_PER_STEP == 0 else 1
    wb = hp * HEAD_DIM
    cq = col_q // wb
    hg = heads // hp
    assert col_q % wb == 0
    return pl.pallas_call(
        functools.partial(_fox_prompt_kernel, tq=tq),
        grid=(n_batch, hg, t // tq),
        in_specs=[pl.BlockSpec((tq, wb), lambda b, h, i: (b * (t // tq) + i, cq + h)),
                  pl.BlockSpec((t, wb), lambda b, h, i: (b, cq + hg + h)),
                  pl.BlockSpec((t, wb), lambda b, h, i: (b, cq + 2 * hg + h)),
                  pl.BlockSpec((1, heads, 1, t), lambda b, h, i: (b, 0, 0, 0))],
        out_specs=pl.BlockSpec((tq, wb), lambda b, h, i: (b * (t // tq) + i, h)),
        out_shape=jax.ShapeDtypeStruct((n_batch * t, heads * HEAD_DIM), BF16),
        compiler_params=_params("parallel", "parallel", "arbitrary"),
        name=name,
    )(p, p, p, nct)


def _online_update(s_parts, v_parts, m_ref, l_ref, acc_ref, first):
    m_new = jnp.max(s_parts[0], axis=1, keepdims=True)
    for s in s_parts[1:]:
        m_new = jnp.maximum(m_new, jnp.max(s, axis=1, keepdims=True))
    if not first:
        m_new = jnp.maximum(m_ref[...], m_new)
    l_new, pv = None, None
    for s, v in zip(s_parts, v_parts):
        p = jnp.exp2(s - m_new)
        l_part = jnp.sum(p, axis=1, keepdims=True)
        pv_part = jnp.dot(p.astype(BF16), v, preferred_element_type=F32)
        l_new = l_part if l_new is None else l_new + l_part
        pv = pv_part if pv is None else pv + pv_part
    if first:
        l_ref[...] = l_new
        acc_ref[...] = pv
    else:
        corr = jnp.exp2(m_ref[...] - m_new)
        l_ref[...] = l_ref[...] * corr + l_new
        acc_ref[...] = acc_ref[...] * corr + pv
    m_ref[...] = m_new


def _scores(q_rows, k_ref):
    s = lax.dot_general(q_rows, k_ref[...].astype(BF16), NT_DIMS, preferred_element_type=F32)
    return s * (ATTN_SCALE * LOG2E)


def _lf_scan_kernel(lf_ref, suf_ref, tot_ref, *, heads):
    lf = lf_ref[...]
    w = lf.shape[1]
    lane = lax.broadcasted_iota(jnp.int32, lf.shape, 1)
    a = lf
    shift = heads
    while shift < w:
        a = a + jnp.where(lane + shift < w, pltpu.roll(a, w - shift, 1), 0.0)
        shift *= 2
    suf_ref[...] = jnp.where(lane + heads < w, pltpu.roll(a, w - heads, 1), 0.0)
    tot = jnp.where(lane < heads, a, 0.0)
    shift = heads
    while shift < w:
        tot = tot + pltpu.roll(tot, shift, 1)
        shift *= 2
    tot_ref[...] = tot


def _lf_scan(lf2, heads, name):
    n_phys, w = lf2.shape
    tp = _pick_tile(n_phys, 256, 8)
    spec = pl.BlockSpec((tp, w), lambda i: (i, 0))
    return pl.pallas_call(
        functools.partial(_lf_scan_kernel, heads=heads),
        grid=(n_phys // tp,),
        in_specs=[spec],
        out_specs=[spec, spec],
        out_shape=[jax.ShapeDtypeStruct((n_phys, w), F32)] * 2,
        compiler_params=_params("parallel"),
        name=name,
    )(lf2)


def _fox_sample_kernel(pt_ref, q_ref, hm_ref, nm_ref, kn_ref, vn_ref, bn_ref, *refs, pps):
    del pt_ref
    k_refs, v_refs = refs[:pps], refs[pps:2 * pps]
    suf_refs, tot_refs = refs[2 * pps:3 * pps], refs[3 * pps:4 * pps]
    o_ref, m_ref, l_ref, acc_ref, carry_ref = refs[4 * pps:]
    pg = pl.program_id(1)
    q_rows = q_ref[0]

    @pl.when(pg == 0)
    def _():
        s = _scores(q_rows, kn_ref.at[0]) + bn_ref[0] * LOG2E + nm_ref[...]
        _online_update([s], [vn_ref[0].astype(BF16)], m_ref, l_ref, acc_ref, first=True)
        carry_ref[...] = jnp.zeros_like(carry_ref)

    carry = carry_ref[...]
    s_parts, v_parts = [], []
    for w in range(pps):
        s_parts.append(_scores(q_rows, k_refs[w]) + (suf_refs[w][0] + carry) * LOG2E + hm_ref[...])
        v_parts.append(v_refs[w][...].astype(BF16))
        carry = carry + tot_refs[w][0]
    carry_ref[...] = carry
    _online_update(s_parts, v_parts, m_ref, l_ref, acc_ref, first=False)

    @pl.when(pg == pl.num_programs(1) - 1)
    def _():
        o_ref[0] = acc_ref[...] / l_ref[...]


def _fox_sample(page_table, q_rows, head_mask, new_mask, k_new, v_new, bias_new, k_pool, v_pool, suf, tot,
                page0, pps, name):
    n_batch, n_pages = page_table.shape
    rows = q_rows.shape[1]
    w = head_mask.shape[1]
    wn = new_mask.shape[1]
    per_b = lambda b, p, pt: (b, 0, 0)
    const = lambda b, p, pt: (0, 0)

    def page(slot, base):
        return lambda b, p, pt: (base + pt[b * n_pages + n_pages - 1 - (p * pps + slot)], 0)

    def page3(slot):
        return lambda b, p, pt: (pt[b * n_pages + n_pages - 1 - (p * pps + slot)], 0, 0)

    grid_spec = pltpu.PrefetchScalarGridSpec(
        num_scalar_prefetch=1,
        grid=(n_batch, n_pages // pps),
        in_specs=[pl.BlockSpec((1, rows, HEAD_DIM), per_b),
                  pl.BlockSpec((rows, w), const), pl.BlockSpec((rows, wn), const),
                  pl.BlockSpec((1, wn, HEAD_DIM), per_b), pl.BlockSpec((1, wn, HEAD_DIM), per_b),
                  pl.BlockSpec((1, 1, wn), per_b)]
                 + [pl.BlockSpec((w, HEAD_DIM), page(s, page0)) for s in range(pps)] * 2
                 + [pl.BlockSpec((1, 1, w), page3(s)) for s in range(pps)] * 2,
        out_specs=pl.BlockSpec((1, rows, HEAD_DIM), per_b),
        scratch_shapes=[pltpu.VMEM((rows, 1), F32), pltpu.VMEM((rows, 1), F32),
                        pltpu.VMEM((rows, HEAD_DIM), F32), pltpu.VMEM((1, w), F32)],
    )
    return pl.pallas_call(
        functools.partial(_fox_sample_kernel, pps=pps),
        grid_spec=grid_spec,
        out_shape=jax.ShapeDtypeStruct((n_batch, rows, HEAD_DIM), F32),
        compiler_params=_params("parallel", "arbitrary"),
        name=name,
    )(page_table.reshape(-1), q_rows, head_mask, new_mask, k_new, v_new, bias_new,
      *([k_pool] * pps), *([v_pool] * pps), *([suf] * pps), *([tot] * pps))


def _rope_kernel(x_ref, c_ref, s_ref, o_ref):
    for hh in range(x_ref.shape[1] // HEAD_DIM):
        sl = slice(hh * HEAD_DIM, (hh + 1) * HEAD_DIM)
        x = x_ref[:, sl]
        o_ref[:, sl] = x * c_ref[...] + pltpu.roll(x, HEAD_DIM // 2, 1) * s_ref[...]


def _rope(p, cos2, sin2, width, name):
    m = p.shape[0]
    tm = _pick_tile(m, ROW_TILE_CAP, 8)
    tc = _pick_tile(width, 512, HEAD_DIM)
    return pl.pallas_call(
        _rope_kernel,
        grid=(m // tm, width // tc),
        in_specs=[pl.BlockSpec((tm, tc), lambda i, j: (i, j)),
                  pl.BlockSpec((tm, HEAD_DIM), lambda i, j: (i, 0)), pl.BlockSpec((tm, HEAD_DIM), lambda i, j: (i, 0))],
        out_specs=pl.BlockSpec((tm, tc), lambda i, j: (i, j)),
        out_shape=jax.ShapeDtypeStruct((m, width), F32),
        compiler_params=_params("parallel", "arbitrary"),
        name=name,
    )(p, cos2, sin2)


def _dot_f32_nt(a, b):
    a3, b3 = _split3(a), _split3(b)
    acc = None
    for ia, ap in enumerate(a3):
        for ib, bp in enumerate(b3):
            if ia + ib > 2:
                continue
            term = lax.dot_general(ap, bp, NT_DIMS, preferred_element_type=F32)
            acc = term if acc is None else acc + term
    return acc


def _topk_mask(gate, n_blocks, topk):
    lane = lax.broadcasted_iota(jnp.int32, gate.shape, 1)
    cnt = jnp.zeros_like(gate)
    for mth in range(n_blocks):
        gm = gate[:, mth:mth + 1]
        ahead = (gm > gate) | ((gm == gate) & (mth < lane))
        cnt = cnt + jnp.where(ahead, 1.0, 0.0)
    return jnp.where((cnt < topk) & (gate > -jnp.inf), 1.0, 0.0)


def _moba_prompt_kernel(q_ref, k_ref, v_ref, o_ref, km_ref, *, blk):
    qi = pl.program_id(2)
    nb = k_ref.shape[0] // blk
    group = q_ref.shape[1] // HEAD_DIM

    @pl.when(qi == 0)
    def _():
        km_ref[...] = jnp.zeros_like(km_ref)
        for n in range(nb):
            km_ref[n:n + 1, :] = jnp.mean(k_ref[n * blk:(n + 1) * blk, :], axis=0, keepdims=True)

    heads = [q_ref[:, r * HEAD_DIM:(r + 1) * HEAD_DIM] for r in range(group)]
    q_sum = heads[0]
    for r in range(1, group):
        q_sum = q_sum + heads[r]
    gate = _dot_f32_nt(q_sum, km_ref[...])
    lane = lax.broadcasted_iota(jnp.int32, gate.shape, 1)
    gate = jnp.where(lane < qi, gate, -jnp.inf)
    drop = (1.0 - _topk_mask(gate, nb - 1, MOBA_TOPK)) * NEG

    for nk in range(1, nb + 1):
        @pl.when(qi == nk - 1)
        def _(nk=nk):
            w = nk * blk
            kb = k_ref[0:w, :].astype(BF16)
            vb = v_ref[0:w, :].astype(BF16)
            bias = [jnp.broadcast_to(drop[:, n:n + 1], (blk, blk)) for n in range(nk - 1)]
            for r in range(group):
                s = lax.dot_general(heads[r].astype(BF16), kb, NT_DIMS, preferred_element_type=F32)
                s = s * (ATTN_SCALE * LOG2E)
                s = _causal_tail(s, blk)
                if nk > 1:
                    s = jnp.concatenate([s[:, n * blk:(n + 1) * blk] + bias[n] for n in range(nk - 1)]
                                        + [s[:, w - blk:]], axis=1)
                o_ref[:, r * HEAD_DIM:(r + 1) * HEAD_DIM] = _softmax_pv(s, vb).astype(o_ref.dtype)


def _moba_prompt(qk, p, n_batch, t, kv_heads, col_v, name):
    blk = MOBA_BLOCK
    gw = MOBA_GROUP * HEAD_DIM
    ck = kv_heads * MOBA_GROUP
    cv = col_v // HEAD_DIM
    assert t // blk <= LANES
    return pl.pallas_call(
        functools.partial(_moba_prompt_kernel, blk=blk),
        grid=(n_batch, kv_heads, t // blk),
        in_specs=[pl.BlockSpec((blk, gw), lambda b, g, i: (b * (t // blk) + i, g)),
                  pl.BlockSpec((t, HEAD_DIM), lambda b, g, i: (b, ck + g)),
                  pl.BlockSpec((t, HEAD_DIM), lambda b, g, i: (b, cv + g))],
        out_specs=pl.BlockSpec((blk, gw), lambda b, g, i: (b * (t // blk) + i, g)),
        out_shape=jax.ShapeDtypeStruct((n_batch * t, kv_heads * gw), BF16),
        scratch_shapes=[pltpu.VMEM((LANES, HEAD_DIM), F32)],
        compiler_params=_params("parallel", "parallel", "arbitrary"),
        name=name,
    )(qk, qk, p)


def _kmean_kernel(pt_ref, *refs, n_in, pages_per_block, kv_heads):
    del pt_ref
    o_ref = refs[n_in]
    blocks = n_in // pages_per_block
    for j in range(blocks):
        tot = None
        for w in range(pages_per_block):
            x_ref = refs[j * pages_per_block + w]
            for tok in range(PAGE_SIZE):
                part = x_ref[tok * kv_heads:(tok + 1) * kv_heads, :]
                tot = part if tot is None else tot + part
        o_ref[0, pl.program_id(1) * blocks + j] = tot / (pages_per_block * PAGE_SIZE)


def _moba_kmean(page_table, k_pool, page0, kv_heads, name):
    n_batch, n_pages = page_table.shape
    ppb = MOBA_BLOCK // PAGE_SIZE
    nb = n_pages // ppb
    blocks = _pick_tile(nb, 4, 1)
    n_in = blocks * ppb
    w = PAGE_SIZE * kv_heads

    def page(slot):
        return lambda b, n, pt: (page0 + pt[b * n_pages + n * n_in + slot], 0)

    grid_spec = pltpu.PrefetchScalarGridSpec(
        num_scalar_prefetch=1,
        grid=(n_batch, nb // blocks),
        in_specs=[pl.BlockSpec((w, HEAD_DIM), page(s)) for s in range(n_in)],
        out_specs=pl.BlockSpec((1, nb, kv_heads, HEAD_DIM), lambda b, n, pt: (b, 0, 0, 0)),
    )
    return pl.pallas_call(
        functools.partial(_kmean_kernel, n_in=n_in, pages_per_block=ppb, kv_heads=kv_heads),
        grid_spec=grid_spec,
        out_shape=jax.ShapeDtypeStruct((n_batch, nb, kv_heads, HEAD_DIM), F32),
        compiler_params=_params("parallel", "arbitrary"),
        name=name,
    )(page_table.reshape(-1), *([k_pool] * n_in))


def _moba_sample_kernel(pt_ref, q_ref, qr_ref, hm_ref, nm_ref, km_ref, kn_ref, vn_ref, *refs,
                        pps, kv_heads, n_new):
    del pt_ref
    k_refs, v_refs = refs[:pps], refs[pps:2 * pps]
    o_ref, m_ref, l_ref, acc_ref, sel_ref = refs[2 * pps:]
    pg = pl.program_id(1)
    q_rows = qr_ref[0]
    rows_per_head = MOBA_GROUP * n_new
    nb = km_ref.shape[2]
    ppb = MOBA_BLOCK // PAGE_SIZE

    @pl.when(pg == 0)
    def _():
        q = q_ref[0]
        for g in range(kv_heads):
            q_sum = q[:, g * MOBA_GROUP * HEAD_DIM:(g * MOBA_GROUP + 1) * HEAD_DIM]
            for r in range(1, MOBA_GROUP):
                c0 = (g * MOBA_GROUP + r) * HEAD_DIM
                q_sum = q_sum + q[:, c0:c0 + HEAD_DIM]
            sel = _topk_mask(_dot_f32_nt(q_sum, km_ref[0, g]), nb, MOBA_TOPK)
            for r in range(MOBA_GROUP):
                r0 = g * rows_per_head + r * n_new
                sel_ref[r0:r0 + n_new, 0:nb] = sel
        s = _scores(q_rows, kn_ref.at[0]) + nm_ref[...]
        _online_update([s], [vn_ref[0].astype(BF16)], m_ref, l_ref, acc_ref, first=True)

    lane = lax.broadcasted_iota(jnp.int32, (q_rows.shape[0], nb), 1)
    s_parts, v_parts = [], []
    for w in range(pps):
        if w % ppb == 0:
            blk = (pg * pps + w) // ppb
            keep = jnp.sum(jnp.where(lane == blk, sel_ref[:, 0:nb], 0.0), axis=1, keepdims=True)
            drop = (1.0 - keep) * NEG
        s_parts.append(_scores(q_rows, k_refs[w]) + drop + hm_ref[...])
        v_parts.append(v_refs[w][...].astype(BF16))
    _online_update(s_parts, v_parts, m_ref, l_ref, acc_ref, first=False)

    @pl.when(pg == pl.num_programs(1) - 1)
    def _():
        o_ref[0] = acc_ref[...] / l_ref[...]


def _moba_sample(page_table, q, q_rows, head_mask, new_mask, k_mean, k_new, v_new, k_pool, v_pool, page0, pps,
                 kv_heads, n_new, name):
    n_batch, n_pages = page_table.shape
    rows = q_rows.shape[1]
    w = head_mask.shape[1]
    wn = new_mask.shape[1]
    nb = k_mean.shape[2]
    assert nb <= LANES and pps % (MOBA_BLOCK // PAGE_SIZE) == 0
    per_b = lambda b, p, pt: (b, 0, 0)
    const = lambda b, p, pt: (0, 0)

    def page(slot):
        return lambda b, p, pt: (page0 + pt[b * n_pages + p * pps + slot], 0)

    grid_spec = pltpu.PrefetchScalarGridSpec(
        num_scalar_prefetch=1,
        grid=(n_batch, n_pages // pps),
        in_specs=[pl.BlockSpec((1, n_new, q.shape[2]), per_b), pl.BlockSpec((1, rows, HEAD_DIM), per_b),
                  pl.BlockSpec((rows, w), const), pl.BlockSpec((rows, wn), const),
                  pl.BlockSpec((1, kv_heads, nb, HEAD_DIM), lambda b, p, pt: (b, 0, 0, 0)),
                  pl.BlockSpec((1, wn, HEAD_DIM), per_b), pl.BlockSpec((1, wn, HEAD_DIM), per_b)]
                 + [pl.BlockSpec((w, HEAD_DIM), page(s)) for s in range(pps)] * 2,
        out_specs=pl.BlockSpec((1, rows, HEAD_DIM), per_b),
        scratch_shapes=[pltpu.VMEM((rows, 1), F32), pltpu.VMEM((rows, 1), F32),
                        pltpu.VMEM((rows, HEAD_DIM), F32), pltpu.VMEM((rows, LANES), F32)],
    )
    return pl.pallas_call(
        functools.partial(_moba_sample_kernel, pps=pps, kv_heads=kv_heads, n_new=n_new),
        grid_spec=grid_spec,
        out_shape=jax.ShapeDtypeStruct((n_batch, rows, HEAD_DIM), F32),
        compiler_params=_params("parallel", "arbitrary"),
        name=name,
    )(page_table.reshape(-1), q, q_rows, head_mask, new_mask, k_mean, k_new, v_new,
      *([k_pool] * pps), *([v_pool] * pps))


def _cross_kernel(q_ref, k_ref, v_ref, o_ref):
    s = lax.dot_general(q_ref[...].astype(BF16), k_ref[...].astype(BF16), NT_DIMS, preferred_element_type=F32)
    s = s * (CROSS_SCALE * LOG2E)
    m = jnp.max(s, axis=1, keepdims=True)
    p = jnp.exp2(s - m)
    l = jnp.sum(p, axis=1, keepdims=True)
    o = jnp.dot(p.astype(BF16), v_ref[...].astype(BF16), preferred_element_type=F32)
    o_ref[...] = (o / l).astype(o_ref.dtype)


def _cross(q, mem_k, mem_v, row0, n_batch, t, out_dtype, name):
    mem = mem_k.shape[0] // n_batch
    tq = _pick_tile(t, 512, 8)
    nq = t // tq
    r0 = row0 // tq
    return pl.pallas_call(
        _cross_kernel,
        grid=(n_batch, nq, CROSS_HEADS),
        in_specs=[pl.BlockSpec((tq, CROSS_HEAD_DIM), lambda b, i, h: (r0 + b * nq + i, h)),
                  pl.BlockSpec((mem, CROSS_HEAD_DIM), lambda b, i, h: (b, h)),
                  pl.BlockSpec((mem, CROSS_HEAD_DIM), lambda b, i, h: (b, h))],
        out_specs=pl.BlockSpec((tq, CROSS_HEAD_DIM), lambda b, i, h: (b * nq + i, h)),
        out_shape=jax.ShapeDtypeStruct((n_batch * t, CROSS_HEADS * CROSS_HEAD_DIM), out_dtype),
        compiler_params=_params("parallel", "parallel", "parallel"),
        name=name,
    )(q, mem_k, mem_v)


def _head_rows(q):
    nb, t, w = q.shape
    return q.reshape(nb, t, w // HEAD_DIM, HEAD_DIM).transpose(0, 2, 1, 3).reshape(nb, -1, HEAD_DIM).astype(BF16)


def _decode_masks(q_heads, kv_heads, n_new):
    row = jnp.arange(q_heads * n_new)
    own = ((row // n_new) // (q_heads // kv_heads))[:, None]
    t_q = (row % n_new)[:, None]
    lane = jnp.arange(PAGE_SIZE * kv_heads)[None, :]
    head_mask = jnp.where(lane % kv_heads == own, 0.0, NEG).astype(F32)
    lane = jnp.arange(n_new * kv_heads)[None, :]
    new_mask = jnp.where((lane % kv_heads == own) & (lane // kv_heads <= t_q), 0.0, NEG).astype(F32)
    return head_mask, new_mask


def kernel(x_prompt, x_sample, cache_fox_k, cache_fox_v, cache_fox_logf, state_conv, cache_moba_k, cache_moba_v, cache_mem_k, cache_mem_v, page_table, mem_prompt, w_in_even, b_forget, w_conv, w_out_even, w_in_odd, w_out_odd, w_cross_q, w_cross_k, w_cross_v, w_cross_o, w_ffa_gate, w_ffa_up, w_ffa_down, w_ffb_gate, w_ffb_up, w_ffb_down, ln_g, ln_b):
    bp, t, d = x_prompt.shape
    bs, ts, _ = x_sample.shape
    depth = ln_g.shape[0]
    mp, ms = bp * t, bs * ts
    m = mp + ms
    alpha = (2 * depth) ** 0.25
    conv_ch = d // 2
    fox_heads = conv_ch // HEAD_DIM
    fox_w = fox_heads * HEAD_DIM
    moba_heads = d // HEAD_DIM
    kv_heads = moba_heads // MOBA_GROUP
    kv_w = kv_heads * HEAD_DIM
    cross_w = CROSS_HEADS * CROSS_HEAD_DIM
    mem_len = mem_prompt.shape[1]
    n_pages = page_table.shape[1]
    past_len = n_pages * PAGE_SIZE
    new_pad = BF16_SUBLANES

    wb = {"in_even": [w_in_even[l:l + 1].astype(BF16) for l in range(w_in_even.shape[0])]}
    wb |= {name: _to_bf16(w, "cast_" + name) for name, w in (
        ("out_even", w_out_even), ("in_odd", w_in_odd), ("out_odd", w_out_odd),
        ("cross_q", w_cross_q), ("cross_k", w_cross_k), ("cross_v", w_cross_v), ("cross_o", w_cross_o),
        ("ffa_gate", w_ffa_gate), ("ffa_up", w_ffa_up), ("ffa_down", w_ffa_down),
        ("ffb_gate", w_ffb_gate), ("ffb_up", w_ffb_up), ("ffb_down", w_ffb_down))}
    def ffn(act, which, layer, g, b, name):
        hidden = _swiglu(act, wb[which + "_gate"], wb[which + "_up"], layer, name + "_swiglu")
        return _resid(hidden, wb[which + "_down"], layer, act, g, b, alpha, 0.5, name + "_down", tn_cap=256)

    pos = jnp.concatenate([jnp.tile(jnp.arange(t, dtype=jnp.int32), bp),
                           jnp.tile(past_len + jnp.arange(ts, dtype=jnp.int32), bs)])
    half = HEAD_DIM // 2
    inv_freq = ROPE_THETA ** (-jnp.arange(half, dtype=F32) / half)
    ang = pos.astype(F32)[:, None] * inv_freq[None, :]
    cos2 = jnp.concatenate([jnp.cos(ang), jnp.cos(ang)], axis=1)
    sin2 = jnp.concatenate([-jnp.sin(ang), jnp.sin(ang)], axis=1)

    ones = jnp.ones((1, d), F32)
    zeros = jnp.zeros((1, d), F32)
    unit_stats = lambda rows: jnp.zeros((rows, LANES), F32).at[:, 1].set(1.0)

    mem_act = (mem_prompt.reshape(bp * mem_len, d), unit_stats(bp * mem_len), ones, zeros)
    mem_k_p = jnp.stack([_proj(mem_act, wb["cross_k"], i, cross_w, F32, f"mem_k{i}") for i in range(depth)])
    mem_v_p = jnp.stack([_proj(mem_act, wb["cross_v"], i, cross_w, F32, f"mem_v{i}") for i in range(depth)])

    x0 = jnp.concatenate([x_prompt.reshape(mp, d), x_sample.reshape(ms, d)], axis=0)
    act = (x0, unit_stats(m), ones, zeros)
    st_p = {name: [] for name in ("fox_k", "fox_v", "fox_logf", "conv", "moba_k", "moba_v")}
    st_s = {name: [] for name in st_p}

    for i in range(depth):
        g = [ln_g[i, s].reshape(1, d) for s in range(4)]
        b = [ln_b[i, s].reshape(1, d) for s in range(4)]
        j = i // 2
        act = ffn(act, "ffa", i, g[0], b[0], f"ffa{i}")

        if i % 2 == 0:
            n_main = 3 * conv_ch + 3 * fox_w
            w_f = jnp.pad(w_in_even[j, :, n_main:].astype(BF16), ((0, 0), (0, LANES - fox_heads)))[None]
            p, f = _proj(act, wb["in_even"], j, n_main, F32, f"in_even{i}", w_aux=w_f)
            b_f = jnp.pad(b_forget[j].astype(F32), (0, LANES - fox_heads)).reshape(1, LANES)
            col_q, col_k, col_v = 3 * conv_ch, 3 * conv_ch + fox_w, 3 * conv_ch + 2 * fox_w

            a_p, conv_p = _conv(p, jnp.zeros((bp, CONV_WIDTH - 1, conv_ch), F32), w_conv[j], 0, bp, t, conv_ch,
                                BF16, f"conv_p{i}")
            a_s, conv_s = _conv(p, state_conv[j], w_conv[j], mp, bs, ts, conv_ch, F32, f"conv_s{i}")
            a_s = a_s.astype(BF16)

            lf_p, nc_p = _logf(f, b_f, 0, bp, t, f"logf_p{i}")
            lf_s, nc_s = _logf(f, b_f, mp, bs, ts, f"logf_s{i}")
            tq = _pick_tile(t, 256, LANES)
            nct = nc_p[:, :fox_heads].reshape(bp, t, fox_heads).transpose(0, 2, 1).reshape(bp, fox_heads, 1, t)
            o_p = _fox_prompt(p, nct, bp, t, fox_heads, col_q, tq, f"fox_p{i}")

            ps = p[mp:]
            q_s = ps[:, col_q:col_q + fox_w].reshape(bs, ts, fox_w)
            k_s = ps[:, col_k:col_k + fox_w].reshape(bs, ts, fox_w)
            v_s = ps[:, col_v:col_v + fox_w].reshape(bs, ts, fox_w)
            head_mask, new_mask = _decode_masks(fox_heads, fox_heads, ts)
            bias_new = nc_s[:, :fox_heads].reshape(bs, 1, ts * fox_heads)
            n_phys = cache_fox_k.shape[1]
            lf_suffix, lf_total = _lf_scan(cache_fox_logf[j].reshape(n_phys, PAGE_SIZE * fox_heads), fox_heads,
                                           f"lf_scan{i}")
            o_s = _fox_sample(page_table, _head_rows(q_s), head_mask, new_mask,
                              k_s.reshape(bs, ts * fox_heads, HEAD_DIM), v_s.reshape(bs, ts * fox_heads, HEAD_DIM),
                              bias_new, cache_fox_k.reshape(-1, HEAD_DIM), cache_fox_v.reshape(-1, HEAD_DIM),
                              lf_suffix.reshape(n_phys, 1, -1), lf_total.reshape(n_phys, 1, -1),
                              j * n_phys, FOX_PAGES_PER_STEP, f"fox_s{i}")
            o_s = o_s.reshape(bs, fox_heads, ts, HEAD_DIM).transpose(0, 2, 1, 3).reshape(ms, fox_w).astype(BF16)

            mix_in = jnp.concatenate([jnp.concatenate([a_p, o_p], axis=1), jnp.concatenate([a_s, o_s], axis=1)], axis=0)
            w_out = wb["out_even"]

            st_p["fox_k"].append(p[:mp, col_k:col_k + fox_w].reshape(bp, t, fox_heads, HEAD_DIM))
            st_p["fox_v"].append(p[:mp, col_v:col_v + fox_w].reshape(bp, t, fox_heads, HEAD_DIM))
            st_p["fox_logf"].append(lf_p[:, :fox_heads].reshape(bp, t, fox_heads))
            st_p["conv"].append(conv_p)
            st_s["fox_k"].append(k_s.reshape(bs, ts, fox_heads, HEAD_DIM))
            st_s["fox_v"].append(v_s.reshape(bs, ts, fox_heads, HEAD_DIM))
            st_s["fox_logf"].append(lf_s[:, :fox_heads].reshape(bs, ts, fox_heads))
            st_s["conv"].append(conv_s)
        else:
            p = _proj(act, wb["in_odd"], j, d + 2 * kv_w, F32, f"in_odd{i}")
            qk_w = d + kv_w
            qk = _rope(p, cos2, sin2, qk_w, f"rope{i}")
            o_p = _moba_prompt(qk, p, bp, t, kv_heads, qk_w, f"moba_p{i}")

            q_s = qk[mp:, :d].reshape(bs, ts, d)
            k_s = qk[mp:, d:].reshape(bs, ts, kv_w)
            v_s = p[mp:, qk_w:].reshape(bs, ts, kv_w)
            head_mask, new_mask = _decode_masks(moba_heads, kv_heads, ts)
            n_phys = cache_moba_k.shape[1]
            k_pool = cache_moba_k.reshape(-1, HEAD_DIM)
            v_pool = cache_moba_v.reshape(-1, HEAD_DIM)
            k_mean = _moba_kmean(page_table, k_pool, j * n_phys, kv_heads, f"moba_kmean{i}").transpose(0, 2, 1, 3)
            o_s = _moba_sample(page_table, q_s, _head_rows(q_s), head_mask, new_mask, k_mean,
                               k_s.reshape(bs, ts * kv_heads, HEAD_DIM), v_s.reshape(bs, ts * kv_heads, HEAD_DIM),
                               k_pool, v_pool, j * n_phys, MOBA_PAGES_PER_STEP, kv_heads, ts, f"moba_s{i}")
            o_s = o_s.reshape(bs, moba_heads, ts, HEAD_DIM).transpose(0, 2, 1, 3).reshape(ms, d).astype(BF16)

            mix_in = jnp.concatenate([o_p, o_s], axis=0)
            w_out = wb["out_odd"]

            st_p["moba_k"].append(qk[:mp, d:].reshape(bp, t, kv_heads, HEAD_DIM))
            st_p["moba_v"].append(p[:mp, qk_w:].reshape(bp, t, kv_heads, HEAD_DIM))
            st_s["moba_k"].append(k_s.reshape(bs, ts, kv_heads, HEAD_DIM))
            st_s["moba_v"].append(v_s.reshape(bs, ts, kv_heads, HEAD_DIM))

        act = _resid(mix_in, w_out, j, act, g[1], b[1], alpha, 1.0, f"mix_out{i}")

        q = _proj(act, wb["cross_q"], i, cross_w, F32, f"cross_q{i}")
        o_p = _cross(q, mem_k_p[i], mem_v_p[i], 0, bp, t, BF16, f"cross_p{i}")
        o_s = _cross(q, cache_mem_k[i].reshape(bs * mem_len, cross_w), cache_mem_v[i].reshape(bs * mem_len, cross_w),
                     mp, bs, ts, F32, f"cross_s{i}").astype(BF16)
        act = _resid(jnp.concatenate([o_p, o_s], axis=0), wb["cross_o"], i, act, g[2], b[2], alpha, 1.0,
                     f"cross_o{i}")

        act = ffn(act, "ffb", i, g[3], b[3], f"ffb{i}")

    y_prompt = _final_norm(act, 0, mp, "y_prompt").reshape(bp, t, d)
    y_sample = _final_norm(act, mp, ms, "y_sample").reshape(bs, ts, d)
    stack = lambda xs: jnp.stack(xs)
    mem_shape = (depth, bp, mem_len, CROSS_HEADS, CROSS_HEAD_DIM)
    return (y_prompt, y_sample,
            stack(st_p["fox_k"]), stack(st_p["fox_v"]), stack(st_p["fox_logf"]), stack(st_p["conv"]),
            stack(st_p["moba_k"]), stack(st_p["moba_v"]),
            mem_k_p.reshape(mem_shape), mem_v_p.reshape(mem_shape),
            stack(st_s["fox_k"]), stack(st_s["fox_v"]), stack(st_s["fox_logf"]), stack(st_s["conv"]),
            stack(st_s["moba_k"]), stack(st_s["moba_v"]))
```
